```python
import math
import jax, jax.numpy as jnp
from jax import lax
import numpy as np

D_MODEL = 4096
BATCH = 4
SEQ = 2048
DEPTH = 2
DEC_BATCH = 8
DEC_SEQ = 1
PAST_LEN = 16384
PAGE_SIZE = 128

N_A_LAYERS = DEPTH // 2
N_B_LAYERS = DEPTH - N_A_LAYERS
RET_HEADS = 16
RET_DK = D_MODEL // RET_HEADS
RET_DV = D_MODEL // RET_HEADS
RET_CHUNK = 128
RET_THETA = 10000.0
DIFF_HEADS = 16
DIFF_DH = D_MODEL // (2 * DIFF_HEADS)
DIFF_DV = 2 * DIFF_DH
ROT_DIM = DIFF_DH // 4
ROPE_THETA = 500000.0
Q_BLOCK = 128
D_FF = 256 * ((8 * D_MODEL // 3 + 255) // 256)
EPS = 1e-6
F32 = jnp.float32

kernel_name = 'yoco_retention_diffattn_macaron_step'


def rms_norm(x, g):
    xf = x.astype(F32)
    y = xf * lax.rsqrt(jnp.mean(xf * xf, axis=-1, keepdims=True) + EPS)
    return (y * g.astype(F32)).astype(x.dtype)


def rope(x, pos, rot_dim, theta):
    half = rot_dim // 2
    inv = 1.0 / (theta ** (jnp.arange(half, dtype=F32) / half))
    ang = pos.astype(F32)[:, None] * inv[None, :]
    shape = (1, pos.shape[0]) + (1,) * (x.ndim - 3) + (half,)
    cos = jnp.cos(ang).reshape(shape)
    sin = jnp.sin(ang).reshape(shape)
    xr = x[..., :rot_dim].astype(F32)
    x1, x2 = xr[..., :half], xr[..., half:]
    rot = jnp.concatenate([x1 * cos - x2 * sin, x2 * cos + x1 * sin], axis=-1).astype(x.dtype)
    return jnp.concatenate([rot, x[..., rot_dim:]], axis=-1)


def swiglu_half_step(x, g, w_gu, w_down):
    h = rms_norm(x, g)
    gate, up = jnp.split(h @ w_gu, 2, axis=-1)
    return x + 0.5 * ((jax.nn.silu(gate) * up) @ w_down)


def ret_log_gamma():
    return jnp.log1p(-(2.0 ** (-5.0 - jnp.arange(RET_HEADS, dtype=F32))))


def ret_project(h, pos, w_qkvg):
    B, S, _ = h.shape
    q, k, v, g = jnp.split(h @ w_qkvg, 4, axis=-1)
    q = rope(q.reshape(B, S, RET_HEADS, RET_DK), pos, RET_DK, RET_THETA)
    k = rope(k.reshape(B, S, RET_HEADS, RET_DK), pos, RET_DK, RET_THETA) * (RET_DK ** -0.5)
    v = v.reshape(B, S, RET_HEADS, RET_DV)
    return q, k, v, g


def ret_chunk(S_prev, q, k, v, log_gamma):
    L = q.shape[1]
    idx = jnp.arange(L, dtype=F32)
    diff = idx[:, None] - idx[None, :]
    dmat = jnp.where(diff >= 0, jnp.exp(jnp.maximum(diff, 0.0)[None] * log_gamma[:, None, None]), 0.0)
    qf, kf, vf = q.astype(F32), k.astype(F32), v.astype(F32)
    scores = jnp.einsum('blhd,bmhd->bhlm', qf, kf) * dmat[None]
    inner = jnp.einsum('bhlm,bmhe->blhe', scores, vf)
    q_decay = jnp.exp((idx + 1.0)[None, :] * log_gamma[:, None])
    cross = jnp.einsum('blhd,bhde->blhe', qf, S_prev) * q_decay.T[None, :, :, None]
    k_decay = jnp.exp((L - 1.0 - idx)[None, :] * log_gamma[:, None])
    S_new = jnp.exp(L * log_gamma)[None, :, None, None] * S_prev + jnp.einsum('blhd,hl,blhe->bhde', kf, k_decay, vf)
    return S_new, inner + cross


def retention_prompt(q, k, v, log_gamma):
    B, S, H, _ = q.shape
    nc = S // RET_CHUNK

    def to_chunks(t):
        return t.reshape(B, nc, RET_CHUNK, H, t.shape[-1]).swapaxes(0, 1)

    S0 = jnp.zeros((B, H, RET_DK, RET_DV), F32)

    def step(S_c, qkv):
        return ret_chunk(S_c, qkv[0], qkv[1], qkv[2], log_gamma)

    S_fin, o = lax.scan(step, S0, (to_chunks(q), to_chunks(k), to_chunks(v)))
    return o.swapaxes(0, 1).reshape(B, S, H, RET_DV), S_fin


def ret_output(o, g, gn_gain, w_o):
    B, S = o.shape[:2]
    mu = jnp.mean(o, axis=-1, keepdims=True)
    var = jnp.mean(jnp.square(o - mu), axis=-1, keepdims=True)
    y = ((o - mu) * lax.rsqrt(var + EPS)).reshape(B, S, D_MODEL) * gn_gain.astype(F32)
    return (jax.nn.silu(g) * y.astype(g.dtype)) @ w_o


def shared_kv(h, pos, g_kv, w_kv, g_k_norm):
    B, S, _ = h.shape
    kv = rms_norm(h, g_kv) @ w_kv
    k, v = jnp.split(kv, [DIFF_HEADS * 2 * DIFF_DH], axis=-1)
    k = rms_norm(k.reshape(B, S, DIFF_HEADS, 2, DIFF_DH), g_k_norm)
    k = rope(k, pos, ROT_DIM, ROPE_THETA)
    return k, v.reshape(B, S, DIFF_HEADS, DIFF_DV)


def diff_query(h, pos, w_q, g_q_norm):
    B, S, _ = h.shape
    q = rms_norm((h @ w_q).reshape(B, S, DIFF_HEADS, 2, DIFF_DH), g_q_norm)
    return rope(q, pos, ROT_DIM, ROPE_THETA)


def diff_attn_prompt(q, k, v, lam):
    B, S = q.shape[:2]
    nb = S // Q_BLOCK
    qs = (q.astype(F32) * (DIFF_DH ** -0.5)).reshape(B, nb, Q_BLOCK, DIFF_HEADS, 2, DIFF_DH).swapaxes(0, 1)
    kpos = jnp.arange(S)
    vf = v.astype(F32)

    def block(args):
        qb, start = args
        s = jnp.einsum('blhcd,bmhcd->bchlm', qb, k, preferred_element_type=F32)
        qpos = start + jnp.arange(Q_BLOCK)
        s = jnp.where(kpos[None, :] <= qpos[:, None], s, -jnp.inf)
        p = jax.nn.softmax(s, axis=-1)
        w = p[:, 0] - lam * p[:, 1]
        return jnp.einsum('bhlm,bmhe->blhe', w, vf)

    o = lax.map(block, (qs, jnp.arange(nb) * Q_BLOCK))
    return o.swapaxes(0, 1).reshape(B, S, DIFF_HEADS, DIFF_DV)


def _online_update(carry, q, k, v, mask):
    m, l, acc = carry
    s = jnp.einsum('blhcd,bmhcd->bchlm', q, k, preferred_element_type=F32)
    if mask is not None:
        s = jnp.where(mask, s, -jnp.inf)
    m_new = jnp.maximum(m, jnp.max(s, axis=-1))
    corr = jnp.exp(m - m_new)
    p = jnp.exp(s - m_new[..., None])
    l = l * corr + jnp.sum(p, axis=-1)
    acc = acc * corr[..., None] + jnp.einsum('bchlm,bmhe->bchle', p, v.astype(F32))
    return m_new, l, acc


def diff_attn_sample(q, k_new, v_new, cache_k, cache_v, page_table, lam):
    Bd, L = q.shape[:2]
    qs = q.astype(F32) * (DIFF_DH ** -0.5)
    init = (jnp.full((Bd, 2, DIFF_HEADS, L), -jnp.inf, F32),
            jnp.zeros((Bd, 2, DIFF_HEADS, L), F32),
            jnp.zeros((Bd, 2, DIFF_HEADS, L, DIFF_DV), F32))

    def page_step(carry, pages):
        kp = cache_k[pages].reshape(Bd, PAGE_SIZE, DIFF_HEADS, 2, DIFF_DH)
        vp = cache_v[pages]
        return _online_update(carry, qs, kp, vp, None), None

    carry, _ = lax.scan(page_step, init, page_table.T)
    causal = jnp.arange(L)[None, :] <= jnp.arange(L)[:, None]
    m, l, acc = _online_update(carry, qs, k_new, v_new, causal)
    o = acc[:, 0] / l[:, 0, ..., None] - lam * (acc[:, 1] / l[:, 1, ..., None])
    return o.transpose(0, 2, 1, 3)


def diff_output(o, g_sub, lam_init, w_o, dtype):
    B, S = o.shape[:2]
    y = rms_norm(o.astype(dtype), g_sub) * (1.0 - lam_init)
    return y.reshape(B, S, D_MODEL) @ w_o


def setup_inputs(seed: int = 0) -> dict:
    key = jax.random.key(seed)
    ks = iter(jax.random.split(key, 40))
    n_pages = PAST_LEN // PAGE_SIZE
    n_pool = (DEC_BATCH * n_pages * 5) // 4
    D = D_MODEL

    def nrm(shape, scale=1.0):
        return jax.random.normal(next(ks), shape, F32) * scale

    def gain(shape):
        return 1.0 + nrm(shape, 0.01)

    inp = {}
    inp['x_prompt'] = nrm((BATCH, SEQ, D))
    inp['x_sample'] = nrm((DEC_BATCH, DEC_SEQ, D))
    inp['state_ret'] = nrm((N_A_LAYERS, DEC_BATCH, RET_HEADS, RET_DK, RET_DV))
    inp['cache_k'] = nrm((n_pool, PAGE_SIZE, DIFF_HEADS, 2 * DIFF_DH))
    inp['cache_v'] = nrm((n_pool, PAGE_SIZE, DIFF_HEADS, DIFF_DV))
    perm = jax.random.permutation(next(ks), n_pool)[:DEC_BATCH * n_pages]
    inp['page_table'] = perm.reshape(DEC_BATCH, n_pages).astype(jnp.int32)
    inp['g_ffn1'] = gain((DEPTH, D))
    inp['w_ffn1_gu'] = nrm((DEPTH, D, 2 * D_FF), D ** -0.5)
    inp['w_ffn1_down'] = nrm((DEPTH, D_FF, D), D_FF ** -0.5)
    inp['g_mix'] = gain((DEPTH, D))
    inp['g_ffn2'] = gain((DEPTH, D))
    inp['w_ffn2_gu'] = nrm((DEPTH, D, 2 * D_FF), D ** -0.5)
    inp['w_ffn2_down'] = nrm((DEPTH, D_FF, D), D_FF ** -0.5)
    inp['w_ret_qkvg'] = nrm((N_A_LAYERS, D, 4 * D), D ** -0.5)
    inp['g_ret_gn'] = gain((N_A_LAYERS, D))
    inp['w_ret_o'] = nrm((N_A_LAYERS, D, D), D ** -0.5)
    inp['g_kv'] = gain((D,))
    inp['w_kv'] = nrm((D, DIFF_HEADS * (2 * DIFF_DH + DIFF_DV)), D ** -0.5)
    inp['g_k_norm'] = gain((DIFF_DH,))
    inp['w_diff_q'] = nrm((N_B_LAYERS, D, DIFF_HEADS * 2 * DIFF_DH), D ** -0.5)
    inp['g_q_norm'] = gain((N_B_LAYERS, DIFF_DH))
    inp['lambda_q1'] = nrm((N_B_LAYERS, DIFF_DH), 0.1)
    inp['lambda_k1'] = nrm((N_B_LAYERS, DIFF_DH), 0.1)
    inp['lambda_q2'] = nrm((N_B_LAYERS, DIFF_DH), 0.1)
    inp['lambda_k2'] = nrm((N_B_LAYERS, DIFF_DH), 0.1)
    inp['g_sub'] = gain((N_B_LAYERS, DIFF_DV))
    inp['w_diff_o'] = nrm((N_B_LAYERS, D, D), D ** -0.5)
    return inp


def reference(x_prompt, x_sample, state_ret, cache_k, cache_v, page_table,
              g_ffn1, w_ffn1_gu, w_ffn1_down, g_mix, g_ffn2, w_ffn2_gu, w_ffn2_down,
              w_ret_qkvg, g_ret_gn, w_ret_o, g_kv, w_kv, g_k_norm,
              w_diff_q, g_q_norm, lambda_q1, lambda_k1, lambda_q2, lambda_k2, g_sub, w_diff_o):
    pos_p = jnp.arange(SEQ)
    pos_s = PAST_LEN + jnp.arange(DEC_SEQ)
    log_gamma = ret_log_gamma()
    xp, xs = x_prompt, x_sample
    ret_p, ret_s = [], []
    kp = vp = ks = vs = None
    for i in range(DEPTH):
        if i == N_A_LAYERS:
            kp, vp = shared_kv(xp, pos_p, g_kv, w_kv, g_k_norm)
            ks, vs = shared_kv(xs, pos_s, g_kv, w_kv, g_k_norm)
        xp = swiglu_half_step(xp, g_ffn1[i], w_ffn1_gu[i], w_ffn1_down[i])
        xs = swiglu_half_step(xs, g_ffn1[i], w_ffn1_gu[i], w_ffn1_down[i])
        if i < N_A_LAYERS:
            a = i
            q, k, v, g = ret_project(rms_norm(xp, g_mix[i]), pos_p, w_ret_qkvg[a])
            o, s_fin = retention_prompt(q, k, v, log_gamma)
            xp = xp + ret_output(o, g, g_ret_gn[a], w_ret_o[a])
            ret_p.append(s_fin)
            q, k, v, g = ret_project(rms_norm(xs, g_mix[i]), pos_s, w_ret_qkvg[a])
            s_new, o = ret_chunk(state_ret[a].astype(F32), q, k, v, log_gamma)
            xs = xs + ret_output(o, g, g_ret_gn[a], w_ret_o[a])
            ret_s.append(s_new)
        else:
            b = i - N_A_LAYERS
            lam_init = 0.8 - 0.6 * math.exp(-0.3 * i)
            lam = (jnp.exp(jnp.sum(lambda_q1[b].astype(F32) * lambda_k1[b].astype(F32)))
                   - jnp.exp(jnp.sum(lambda_q2[b].astype(F32) * lambda_k2[b].astype(F32))) + lam_init)
            q = diff_query(rms_norm(xp, g_mix[i]), pos_p, w_diff_q[b], g_q_norm[b])
            o = diff_attn_prompt(q, kp, vp, lam)
            xp = xp + diff_output(o, g_sub[b], lam_init, w_diff_o[b], xp.dtype)
            q = diff_query(rms_norm(xs, g_mix[i]), pos_s, w_diff_q[b], g_q_norm[b])
            o = diff_attn_sample(q, ks, vs, cache_k, cache_v, page_table, lam)
            xs = xs + diff_output(o, g_sub[b], lam_init, w_diff_o[b], xs.dtype)
        xp = swiglu_half_step(xp, g_ffn2[i], w_ffn2_gu[i], w_ffn2_down[i])
        xs = swiglu_half_step(xs, g_ffn2[i], w_ffn2_gu[i], w_ffn2_down[i])
    state_ret_prompt = jnp.stack(ret_p)
    state_ret_sample = jnp.stack(ret_s)
    k_prompt = kp.reshape(BATCH, SEQ, DIFF_HEADS, 2 * DIFF_DH)
    k_sample = ks.reshape(DEC_BATCH, DEC_SEQ, DIFF_HEADS, 2 * DIFF_DH)
    return (xp, xs, state_ret_prompt, state_ret_sample, k_prompt, vp, k_sample, vs)
```

```python
import functools
import math

import jax
import jax.numpy as jnp
from jax import lax
from jax.experimental import pallas as pl
from jax.experimental.pallas import tpu as pltpu

F32 = jnp.float32
BF16 = jnp.bfloat16

D_MODEL = 4096
SEQ = 2048
DEPTH = 2
PAST_LEN = 16384
PAGE_SIZE = 128
N_A_LAYERS = DEPTH // 2
RET_HEADS = 16
RET_DK = D_MODEL // RET_HEADS
RET_CHUNK = 128
RET_THETA = 10000.0
DIFF_HEADS = 16
DIFF_DH = D_MODEL // (2 * DIFF_HEADS)
DIFF_DV = 2 * DIFF_DH
ROT_DIM = DIFF_DH // 4
ROPE_THETA = 500000.0
D_FF = 256 * ((8 * D_MODEL // 3 + 255) // 256)
EPS = 1e-6

LANES = 128
VMEM_LIMIT_BYTES = 56 * 1024 * 1024
CAST_ROWS = 128


def _params(n_grid_dims):
    return pltpu.CompilerParams(
        dimension_semantics=("arbitrary",) * n_grid_dims,
        vmem_limit_bytes=VMEM_LIMIT_BYTES)


def _silu(x):
    return x * jax.nn.sigmoid(x)


def _rmsnorm_body(x_ref, g_ref, o_ref):
    x = x_ref[...].astype(F32)
    y = x * lax.rsqrt(jnp.mean(x * x, axis=-1, keepdims=True) + EPS)
    o_ref[...] = (y * g_ref[...]).astype(o_ref.dtype)


def rmsnorm_cast(x, g, out_dtype):
    M, D = x.shape
    tm = min(M, 512)
    return pl.pallas_call(
        _rmsnorm_body,
        grid=(M // tm,),
        in_specs=[pl.BlockSpec((tm, D), lambda i: (i, 0)),
                  pl.BlockSpec((1, D), lambda i: (0, 0))],
        out_specs=pl.BlockSpec((tm, D), lambda i: (i, 0)),
        out_shape=jax.ShapeDtypeStruct((M, D), out_dtype),
        compiler_params=_params(1),
        name="rmsnorm_cast",
    )(x, g.reshape(1, D).astype(F32))


def _ws_matmul_body(*refs, n_w, n_extra, n_out, tnw, tk, epilogue):
    x_ref = refs[0]
    w_refs = refs[1:1 + n_w]
    extra_refs = refs[1 + n_w:1 + n_w + n_extra]
    out_refs = refs[1 + n_w + n_extra:1 + n_w + n_extra + n_out]
    wbf_ref = refs[-1]

    @pl.when(pl.program_id(1) == 0)
    def _():
        rows = min(CAST_ROWS, tk)
        assert tk % rows == 0

        def cast_rows(r, carry):
            sl = pl.ds(pl.multiple_of(r * rows, rows), rows)
            for t, w_ref in enumerate(w_refs):
                wbf_ref[sl, t * tnw:(t + 1) * tnw] = w_ref[sl, :].astype(BF16)
            return carry

        lax.fori_loop(0, tk // rows, cast_rows, 0)

    acc = jnp.dot(x_ref[...].astype(BF16), wbf_ref[...], preferred_element_type=F32)
    epilogue(acc, extra_refs, out_refs)


def ws_matmul(x, w, col_block_offsets, n_tiles, tm, tnw, epilogue, outs,
              extras=(), k_block=0, tk=None, name="ws_matmul"):
    M = x.shape[0]
    tk = x.shape[1] if tk is None else tk
    n_w = len(col_block_offsets)
    in_specs = [pl.BlockSpec((tm, tk), lambda j, i: (i, k_block))]
    for off in col_block_offsets:
        in_specs.append(pl.BlockSpec((tk, tnw), lambda j, i, off=off: (k_block, off + j)))
    for _, bshape, imap in extras:
        in_specs.append(pl.BlockSpec(bshape, imap))
    body = functools.partial(_ws_matmul_body, n_w=n_w, n_extra=len(extras), n_out=len(outs),
                             tnw=tnw, tk=tk, epilogue=epilogue)
    res = pl.pallas_call(
        body,
        grid=(n_tiles, M // tm),
        in_specs=in_specs,
        out_specs=[pl.BlockSpec(bshape, imap) for _, bshape, imap in outs],
        out_shape=[s for s, _, _ in outs],
        scratch_shapes=[pltpu.VMEM((tk, n_w * tnw), BF16)],
        compiler_params=_params(2),
        name=name,
    )(x, *([w] * n_w), *[a for a, _, _ in extras])
    return res


def _row_tile(M):
    return min(M, 1024)


def ffn_gate_up(h, w_gu, out_dtype):
    M = h.shape[0]
    d_ff = w_gu.shape[1] // 2
    tnw = 256
    tm = _row_tile(M)
    n_tiles = d_ff // tnw

    def epilogue(acc, extra_refs, out_refs):
        gate, up = acc[:, :tnw], acc[:, tnw:]
        out_refs[0][...] = (_silu(gate) * up).astype(out_refs[0].dtype)

    out = (jax.ShapeDtypeStruct((M, d_ff), out_dtype), (tm, tnw), lambda j, i: (i, j))
    return ws_matmul(h, w_gu, (0, n_tiles), n_tiles, tm, tnw, epilogue, [out],
                     name="ffn_gate_up")[0]


def matmul_residual(a, w, res, alpha, k_splits=1, name="matmul_residual"):
    M, K = a.shape
    N = w.shape[1]
    tm = min(M, 512)
    tn = 512
    tk = K // k_splits

    def epilogue(acc, extra_refs, out_refs):
        out_refs[0][...] = extra_refs[0][...] + alpha * acc

    out = (jax.ShapeDtypeStruct((M, N), F32), (tm, tn), lambda j, i: (i, j))
    for kb in range(k_splits):
        extra = (res, (tm, tn), lambda j, i: (i, j))
        res = ws_matmul(a, w, (0,), N // tn, tm, tn, epilogue, [out], extras=[extra],
                        k_block=kb, tk=tk, name=name)[0]
    return res


def swiglu_half_step(x, g, w_gu, w_down, act_dtype):
    h = rmsnorm_cast(x, g, act_dtype)
    u = ffn_gate_up(h, w_gu, act_dtype)
    return matmul_residual(u, w_down, x, 0.5, k_splits=2, name="ffn_down")


def _rope_tables_full(pos, rot_dim, theta):
    half = rot_dim // 2
    inv = 1.0 / (theta ** (jnp.arange(half, dtype=F32) / half))
    ang = pos.astype(F32)[:, None] * inv[None, :]
    return jnp.cos(ang), jnp.sin(ang)


def _rope_tables_partial(pos, rot_dim, theta, width):
    half = rot_dim // 2
    cos, sin = _rope_tables_full(pos, rot_dim, theta)
    P = pos.shape[0]
    ones = jnp.ones((P, width - rot_dim), F32)
    zeros_rest = jnp.zeros((P, width - rot_dim), F32)
    zeros_half = jnp.zeros((P, half), F32)
    c = jnp.concatenate([cos, cos, ones], axis=1)
    s_lo = jnp.concatenate([-sin, zeros_half, zeros_rest], axis=1)
    s_hi = jnp.concatenate([zeros_half, sin, zeros_rest], axis=1)
    return c, s_lo, s_hi


def ret_project(h, w_qkvg, pos, act_dtype):
    M, D = h.shape
    tm = _row_tile(M)
    tn = 2 * RET_DK
    half = RET_DK // 2
    cos, sin = _rope_tables_full(pos, RET_DK, RET_THETA)
    P = cos.shape[0]
    n_pos_blocks = max(P // tm, 1)
    k_scale = RET_DK ** -0.5
    n_head_tiles = D // tn

    def rope_epilogue(acc, extra_refs, out_refs):
        c = extra_refs[0][...]
        s = extra_refs[1][...]
        is_k = pl.program_id(0) >= n_head_tiles
        scale = jnp.where(is_k, k_scale, 1.0).astype(F32)
        for hh in range(tn // RET_DK):
            x1 = acc[:, hh * RET_DK:hh * RET_DK + half]
            x2 = acc[:, hh * RET_DK + half:(hh + 1) * RET_DK]
            out_refs[0][:, hh * RET_DK:hh * RET_DK + half] = (
                (x1 * c - x2 * s) * scale).astype(out_refs[0].dtype)
            out_refs[0][:, hh * RET_DK + half:(hh + 1) * RET_DK] = (
                (x2 * c + x1 * s) * scale).astype(out_refs[0].dtype)

    def plain_epilogue(acc, extra_refs, out_refs):
        out_refs[0][...] = acc.astype(out_refs[0].dtype)

    tab = lambda a: (a, (tm, half), lambda j, i: (i % n_pos_blocks, 0))
    qk = ws_matmul(h, w_qkvg, (0,), 2 * n_head_tiles, tm, tn, rope_epilogue,
                   [(jax.ShapeDtypeStruct((M, 2 * D), act_dtype), (tm, tn), lambda j, i: (i, j))],
                   extras=[tab(cos), tab(sin)], name="ret_qk_proj")[0]
    v = ws_matmul(h, w_qkvg, (2 * n_head_tiles,), n_head_tiles, tm, tn, plain_epilogue,
                  [(jax.ShapeDtypeStruct((M, D), act_dtype), (tm, tn), lambda j, i: (i, j))],
                  name="ret_v_proj")[0]
    g = ws_matmul(h, w_qkvg, (3 * n_head_tiles,), n_head_tiles, tm, tn, plain_epilogue,
                  [(jax.ShapeDtypeStruct((M, D), F32), (tm, tn), lambda j, i: (i, j))],
                  name="ret_g_proj")[0]
    return qk, v, g


def _log_gamma_rows():
    lg = jnp.log1p(-(2.0 ** (-5.0 - jnp.arange(RET_HEADS, dtype=F32))))
    return jnp.broadcast_to(lg[:, None, None], (RET_HEADS, 1, RET_DK))


def _group_norm_gate(o, gn, g):
    mu = jnp.mean(o, axis=-1, keepdims=True)
    d = o - mu
    var = jnp.mean(d * d, axis=-1, keepdims=True)
    y = d * lax.rsqrt(var + EPS) * gn
    return _silu(g) * y


def _ret_prompt_body(q_ref, k_ref, v_ref, g_ref, lg_ref, gn_ref, y_ref, s_ref, state_ref, *, n_chunks):
    L = RET_CHUNK
    lg = lg_ref[0]
    r = lax.broadcasted_iota(jnp.int32, (L, L), 0)
    c = lax.broadcasted_iota(jnp.int32, (L, L), 1)
    diff = (r - c).astype(F32)
    dmat = jnp.where(diff >= 0, jnp.exp(jnp.maximum(diff, 0.0) * lg[:, :L]), 0.0)
    rows = lax.broadcasted_iota(jnp.int32, (L, RET_DK), 0).astype(F32)
    q_decay = jnp.exp((rows + 1.0) * lg)
    k_decay = jnp.exp((L - 1.0 - rows) * lg)
    chunk_decay = jnp.exp(float(L) * lg)
    gn = gn_ref[...]
    state_ref[...] = jnp.zeros_like(state_ref)

    def chunk(ci, carry):
        sl = pl.ds(pl.multiple_of(ci * L, L), L)
        q = q_ref[sl, :]
        k = k_ref[sl, :]
        v = v_ref[sl, :]
        state = state_ref[...]
        scores = lax.dot_general(q, k, (((1,), (1,)), ((), ())), preferred_element_type=F32) * dmat
        inner = jnp.dot(scores.astype(BF16), v, preferred_element_type=F32)
        cross = jnp.dot(q, state.astype(BF16), preferred_element_type=F32) * q_decay
        o = inner + cross
        kd = (k.astype(F32) * k_decay).astype(BF16)
        state_ref[...] = chunk_decay * state + lax.dot_general(
            kd, v, (((0,), (0,)), ((), ())), preferred_element_type=F32)
        y_ref[sl, :] = _group_norm_gate(o, gn, g_ref[sl, :]).astype(y_ref.dtype)
        return carry

    lax.fori_loop(0, n_chunks, chunk, 0)
    s_ref[0, 0] = state_ref[...]


def retention_prompt(qk, v, g, gn_gain, batch, seq):
    M, D = v.shape
    H, DK = RET_HEADS, RET_DK
    body = functools.partial(_ret_prompt_body, n_chunks=seq // RET_CHUNK)
    y, s_fin = pl.pallas_call(
        body,
        grid=(batch, H),
        in_specs=[pl.BlockSpec((seq, DK), lambda b, h: (b, h)),
                  pl.BlockSpec((seq, DK), lambda b, h: (b, H + h)),
                  pl.BlockSpec((seq, DK), lambda b, h: (b, h)),
                  pl.BlockSpec((seq, DK), lambda b, h: (b, h)),
                  pl.BlockSpec((1, 1, DK), lambda b, h: (h, 0, 0)),
                  pl.BlockSpec((1, DK), lambda b, h: (0, h))],
        out_specs=[pl.BlockSpec((seq, DK), lambda b, h: (b, h)),
                   pl.BlockSpec((1, 1, DK, DK), lambda b, h: (b, h, 0, 0))],
        out_shape=[jax.ShapeDtypeStruct((M, D), BF16),
                   jax.ShapeDtypeStruct((batch, H, DK, DK), F32)],
        scratch_shapes=[pltpu.VMEM((DK, DK), F32)],
        compiler_params=_params(2),
        name="retention_prompt",
    )(qk, qk, v, g, _log_gamma_rows(), gn_gain.reshape(1, D).astype(F32))
    return y, s_fin


def _ret_step_body(q_ref, k_ref, v_ref, g_ref, lg_ref, gn_ref, s_in_ref, y_ref, s_out_ref, *, batch):
    lg = lg_ref[0]
    d0 = jnp.exp(0.0 * lg)
    q_decay = jnp.exp(1.0 * lg)
    k_decay = jnp.exp(0.0 * lg)
    step_decay = jnp.exp(1.0 * lg)
    rnd = lambda t: t.astype(BF16).astype(F32)
    qf, kf, vf = rnd(q_ref[...]), rnd(k_ref[...]), rnd(v_ref[...])
    row = lax.broadcasted_iota(jnp.int32, qf.shape, 0)
    scores = jnp.sum(qf * kf, axis=-1, keepdims=True) * d0[:, :1]
    inner = rnd(scores) * vf
    kd = rnd(kf * k_decay)
    cross = jnp.zeros(qf.shape, F32)
    for b in range(batch):
        state = s_in_ref[b, 0].astype(F32)
        q_only_b = jnp.where(row == b, qf, 0.0)
        k_only_b = jnp.where(row == b, kd, 0.0)
        cross = cross + jnp.dot(q_only_b, rnd(state), preferred_element_type=F32)
        s_out_ref[b, 0] = step_decay * state + lax.dot_general(
            k_only_b, vf, (((0,), (0,)), ((), ())), preferred_element_type=F32)
    o = inner + cross * q_decay
    y_ref[...] = _group_norm_gate(o, gn_ref[...], g_ref[...]).astype(y_ref.dtype)


def retention_step(qk, v, g, gn_gain, state):
    B, D = v.shape
    H, DK = RET_HEADS, RET_DK
    body = functools.partial(_ret_step_body, batch=B)
    y, s_new = pl.pallas_call(
        body,
        grid=(H,),
        in_specs=[pl.BlockSpec((B, DK), lambda h: (0, h)),
                  pl.BlockSpec((B, DK), lambda h: (0, H + h)),
                  pl.BlockSpec((B, DK), lambda h: (0, h)),
                  pl.BlockSpec((B, DK), lambda h: (0, h)),
                  pl.BlockSpec((1, 1, DK), lambda h: (h, 0, 0)),
                  pl.BlockSpec((1, DK), lambda h: (0, h)),
                  pl.BlockSpec((B, 1, DK, DK), lambda h: (0, h, 0, 0))],
        out_specs=[pl.BlockSpec((B, DK), lambda h: (0, h)),
                   pl.BlockSpec((B, 1, DK, DK), lambda h: (0, h, 0, 0))],
        out_shape=[jax.ShapeDtypeStruct((B, D), F32),
                   jax.ShapeDtypeStruct((B, H, DK, DK), F32)],
        compiler_params=_params(1),
        name="retention_step",
    )(qk, qk, v, g, _log_gamma_rows(), gn_gain.reshape(1, D).astype(F32), state)
    return y, s_new


def _norm_rope_groups(acc, gain, c, s_lo, s_hi, n_groups):
    half = ROT_DIM // 2
    outs = []
    for gi in range(n_groups):
        xg = acc[:, gi * DIFF_DH:(gi + 1) * DIFF_DH]
        yg = xg * lax.rsqrt(jnp.mean(xg * xg, axis=-1, keepdims=True) + EPS) * gain
        rot = (yg * c + pltpu.roll(yg, DIFF_DH - half, 1) * s_lo + pltpu.roll(yg, half, 1) * s_hi)
        outs.append(rot)
    return outs


def _diff_rope_extras(pos, tm):
    c, s_lo, s_hi = _rope_tables_partial(pos, ROT_DIM, ROPE_THETA, DIFF_DH)
    n_pos_blocks = max(pos.shape[0] // tm, 1)
    return [(a, (tm, DIFF_DH), lambda j, i: (i % n_pos_blocks, 0)) for a in (c, s_lo, s_hi)]


def shared_kv(x, g_kv, w_kv, g_k_norm, pos, act_dtype):
    M, D = x.shape
    h = rmsnorm_cast(x, g_kv, act_dtype)
    tm = _row_tile(M)
    tn = 4 * DIFF_DH
    n_groups = tn // DIFF_DH
    nk = (DIFF_HEADS * 2 * DIFF_DH) // tn
    nv = (DIFF_HEADS * DIFF_DV) // tn

    def k_epilogue(acc, extra_refs, out_refs):
        gain, c, s_lo, s_hi = (r[...] for r in extra_refs)
        for gi, rot in enumerate(_norm_rope_groups(acc, gain, c, s_lo, s_hi, n_groups)):
            out_refs[0][:, gi * DIFF_DH:(gi + 1) * DIFF_DH] = rot
            out_refs[1][:, gi * DIFF_DH:(gi + 1) * DIFF_DH] = rot.astype(out_refs[1].dtype)

    def v_epilogue(acc, extra_refs, out_refs):
        out_refs[0][...] = acc
        out_refs[1][...] = acc.astype(out_refs[1].dtype)

    gain = (g_k_norm.reshape(1, DIFF_DH).astype(F32), (1, DIFF_DH), lambda j, i: (0, 0))
    two_outs = lambda n: [(jax.ShapeDtypeStruct((M, n), F32), (tm, tn), lambda j, i: (i, j)),
                          (jax.ShapeDtypeStruct((M, n), act_dtype), (tm, tn), lambda j, i: (i, j))]
    k32, kact = ws_matmul(h, w_kv, (0,), nk, tm, tn, k_epilogue, two_outs(nk * tn),
                          extras=[gain] + _diff_rope_extras(pos, tm), name="shared_k_proj")
    v32, vact = ws_matmul(h, w_kv, (nk,), nv, tm, tn, v_epilogue, two_outs(nv * tn),
                          name="shared_v_proj")
    return k32, kact, v32, vact


def diff_query(h, w_q, g_q_norm, pos, act_dtype):
    M, D = h.shape
    tm = _row_tile(M)
    tn = 4 * DIFF_DH
    n_groups = tn // DIFF_DH
    q_scale = DIFF_DH ** -0.5

    def q_epilogue(acc, extra_refs, out_refs):
        gain, c, s_lo, s_hi = (r[...] for r in extra_refs)
        for gi, rot in enumerate(_norm_rope_groups(acc, gain, c, s_lo, s_hi, n_groups)):
            out_refs[0][:, gi * DIFF_DH:(gi + 1) * DIFF_DH] = (rot * q_scale).astype(out_refs[0].dtype)

    gain = (g_q_norm.reshape(1, DIFF_DH).astype(F32), (1, DIFF_DH), lambda j, i: (0, 0))
    N = w_q.shape[1]
    return ws_matmul(h, w_q, (0,), N // tn, tm, tn, q_epilogue,
                     [(jax.ShapeDtypeStruct((M, N), act_dtype), (tm, tn), lambda j, i: (i, j))],
                     extras=[gain] + _diff_rope_extras(pos, tm), name="diff_q_proj")[0]


def _lambda_value(lam_ref, lam_init):
    l = lam_ref[...]
    a = jnp.sum(l[0:1, :] * l[1:2, :], axis=-1, keepdims=True)
    b = jnp.sum(l[2:3, :] * l[3:4, :], axis=-1, keepdims=True)
    return jnp.exp(a) - jnp.exp(b) + lam_init


def _sub_norm(o, g_sub, lam_init):
    y = o * lax.rsqrt(jnp.mean(o * o, axis=-1, keepdims=True) + EPS) * g_sub
    return y * (1.0 - lam_init)


def _diff_attn_body(q_ref, k_ref, v_ref, lam_ref, gsub_ref, o_ref, m_ref, l_ref, acc_ref, *, tq, lam_init):
    qi = pl.program_id(2)
    tk = tq
    DH, DV = DIFF_DH, DIFF_DV
    m_ref[...] = jnp.full(m_ref.shape, -jnp.inf, F32)
    l_ref[...] = jnp.zeros(l_ref.shape, F32)
    acc_ref[...] = jnp.zeros(acc_ref.shape, F32)

    def update(j, masked):
        ksl = pl.ds(pl.multiple_of(j * tk, tk), tk)
        kblk = k_ref[ksl, :]
        vblk = v_ref[ksl, :]
        for c in range(2):
            s = lax.dot_general(q_ref[:, c * DH:(c + 1) * DH], kblk[:, c * DH:(c + 1) * DH],
                                (((1,), (1,)), ((), ())), preferred_element_type=F32)
            if masked:
                rr = lax.broadcasted_iota(jnp.int32, (tq, tk), 0)
                cc = lax.broadcasted_iota(jnp.int32, (tq, tk), 1)
                s = jnp.where(cc <= rr, s, -jnp.inf)
            m_prev = m_ref[c]
            m_new = jnp.maximum(m_prev, jnp.max(s, axis=-1, keepdims=True))
            corr = jnp.exp(m_prev - m_new)
            p = jnp.exp(s - m_new[:, :1])
            l_ref[c] = l_ref[c] * corr + jnp.sum(p, axis=-1, keepdims=True)
            acc_ref[c] = acc_ref[c] * corr[:, :1] + jnp.dot(p.astype(BF16), vblk, preferred_element_type=F32)
            m_ref[c] = m_new

    def body(j, carry):
        update(j, False)
        return carry

    lax.fori_loop(0, qi, body, 0)
    update(qi, True)

    lam = _lambda_value(lam_ref, lam_init)
    o = acc_ref[0] / l_ref[0][:, :1] - lam * (acc_ref[1] / l_ref[1][:, :1])
    o_ref[...] = _sub_norm(o, gsub_ref[...], lam_init).astype(o_ref.dtype)


def diff_attn_prompt(q, k, v, lam_rows, g_sub, lam_init, batch, seq, tq=512):
    M, D = q.shape
    H, DV = DIFF_HEADS, DIFF_DV
    tq = min(seq, tq)
    nq = seq // tq
    body = functools.partial(_diff_attn_body, tq=tq, lam_init=lam_init)
    return pl.pallas_call(
        body,
        grid=(batch, H, nq),
        in_specs=[pl.BlockSpec((tq, DV), lambda b, h, i: (b * nq + i, h)),
                  pl.BlockSpec((seq, DV), lambda b, h, i: (b, h)),
                  pl.BlockSpec((seq, DV), lambda b, h, i: (b, h)),
                  pl.BlockSpec((4, DIFF_DH), lambda b, h, i: (0, 0)),
                  pl.BlockSpec((1, DV), lambda b, h, i: (0, 0))],
        out_specs=pl.BlockSpec((tq, DV), lambda b, h, i: (b * nq + i, h)),
        out_shape=jax.ShapeDtypeStruct((M, D), BF16),
        scratch_shapes=[pltpu.VMEM((2, tq, LANES), F32),
                        pltpu.VMEM((2, tq, LANES), F32),
                        pltpu.VMEM((2, tq, DV), F32)],
        compiler_params=_params(3),
        name="diff_attn_prompt",
    )(q, k, v, lam_rows, g_sub.reshape(1, DV).astype(F32))


def _diff_decode_body(pt_ref, qbd_ref, *refs, pages_per_step, lam_init):
    k_refs = refs[:pages_per_step]
    v_refs = refs[pages_per_step:2 * pages_per_step]
    knew_ref, vnew_ref, lam_ref, gsub_ref, o_ref, kbf_ref, vbf_ref, m_ref, l_ref, acc_ref = refs[2 * pages_per_step:]
    b = pl.program_id(0)
    step = pl.program_id(1)
    n_steps = pl.num_programs(1)
    H, DV = DIFF_HEADS, DIFF_DV
    NQ = 2 * H
    T = pages_per_step * PAGE_SIZE

    @pl.when(step == 0)
    def _():
        m_ref[...] = jnp.full(m_ref.shape, -jnp.inf, F32)
        l_ref[...] = jnp.zeros(l_ref.shape, F32)
        acc_ref[...] = jnp.zeros(acc_ref.shape, F32)

    qbd = qbd_ref[0]
    row_head = lax.broadcasted_iota(jnp.int32, (NQ, DV), 0) % H

    def update(n_tok, valid):
        s = jnp.dot(kbf_ref[0:n_tok, :], qbd, preferred_element_type=F32)
        st = s.T[0:NQ, :]
        if valid is not None:
            st = jnp.where(valid, st, -jnp.inf)
        m_prev = m_ref[...]
        m_new = jnp.maximum(m_prev, jnp.max(st, axis=-1, keepdims=True))
        corr = jnp.exp(m_prev - m_new)
        p = jnp.exp(st - m_new[:, :1])
        l_ref[...] = l_ref[...] * corr + jnp.sum(p, axis=-1, keepdims=True)
        full = jnp.dot(p.astype(BF16), vbf_ref[0:n_tok, :], preferred_element_type=F32)
        contrib = jnp.zeros((NQ, DV), F32)
        for h in range(H):
            contrib = contrib + jnp.where(row_head == h, full[:, h * DV:(h + 1) * DV], 0.0)
        acc_ref[...] = acc_ref[...] * corr[:, :1] + contrib
        m_ref[...] = m_new

    for r in range(pages_per_step):
        kbf_ref[r * PAGE_SIZE:(r + 1) * PAGE_SIZE, :] = k_refs[r][0].astype(BF16)
        vbf_ref[r * PAGE_SIZE:(r + 1) * PAGE_SIZE, :] = v_refs[r][0].astype(BF16)
    update(T, None)

    @pl.when(step == n_steps - 1)
    def _():
        nb = knew_ref.shape[0]
        pad = jnp.zeros((PAGE_SIZE - nb, knew_ref.shape[1]), F32)
        kbf_ref[0:PAGE_SIZE, :] = jnp.concatenate([knew_ref[...], pad], axis=0).astype(BF16)
        vbf_ref[0:PAGE_SIZE, :] = jnp.concatenate([vnew_ref[...], pad], axis=0).astype(BF16)
        tok = lax.broadcasted_iota(jnp.int32, (NQ, PAGE_SIZE), 1)
        update(PAGE_SIZE, tok == b)
        lam = _lambda_value(lam_ref, lam_init)
        acc = acc_ref[...]
        l = l_ref[...]
        o = acc[0:H] / l[0:H, :1] - lam * (acc[H:NQ] / l[H:NQ, :1])
        o_ref[0] = _sub_norm(o, gsub_ref[...], lam_init)


def diff_attn_decode(q, k_new, v_new, cache_k, cache_v, page_table, lam_rows, g_sub, lam_init):
    B, D = q.shape
    H, DH, DV = DIFF_HEADS, DIFF_DH, DIFF_DV
    n_pool = cache_k.shape[0]
    n_pages = page_table.shape[1]
    pages_per_step = 2
    n_steps = n_pages // pages_per_step
    ck = cache_k.reshape(n_pool, PAGE_SIZE, D)
    cv = cache_v.reshape(n_pool, PAGE_SIZE, H * DV)
    feat = jnp.arange(D)
    col_of_feat = ((feat // DH) % 2) * H + feat // (2 * DH)
    qbd = jnp.where(col_of_feat[None, :, None] == jnp.arange(LANES)[None, None, :],
                    q[:, :, None], 0.0).astype(BF16)

    def page_spec(r):
        return pl.BlockSpec((1, PAGE_SIZE, D), lambda b, s, pt: (pt[b, s * pages_per_step + r], 0, 0))

    body = functools.partial(_diff_decode_body, pages_per_step=pages_per_step, lam_init=lam_init)
    T = pages_per_step * PAGE_SIZE
    grid_spec = pltpu.PrefetchScalarGridSpec(
        num_scalar_prefetch=1,
        grid=(B, n_steps),
        in_specs=[pl.BlockSpec((1, D, LANES), lambda b, s, pt: (b, 0, 0))]
                 + [page_spec(r) for r in range(pages_per_step)]
                 + [page_spec(r) for r in range(pages_per_step)]
                 + [pl.BlockSpec((B, D), lambda b, s, pt: (0, 0)),
                    pl.BlockSpec((B, H * DV), lambda b, s, pt: (0, 0)),
                    pl.BlockSpec((4, DH), lambda b, s, pt: (0, 0)),
                    pl.BlockSpec((1, DV), lambda b, s, pt: (0, 0))],
        out_specs=pl.BlockSpec((1, H, DV), lambda b, s, pt: (b, 0, 0)),
        scratch_shapes=[pltpu.VMEM((T, D), BF16),
                        pltpu.VMEM((T, H * DV), BF16),
                        pltpu.VMEM((2 * H, LANES), F32),
                        pltpu.VMEM((2 * H, LANES), F32),
                        pltpu.VMEM((2 * H, DV), F32)],
    )
    out = pl.pallas_call(
        body,
        grid_spec=grid_spec,
        out_shape=jax.ShapeDtypeStruct((B, H, DV), F32),
        compiler_params=_params(2),
        name="diff_attn_decode",
    )(page_table, qbd, *([ck] * pages_per_step), *([cv] * pages_per_step),
      k_new, v_new, lam_rows, g_sub.reshape(1, DV).astype(F32))
    return out.reshape(B, H * DV)


def kernel(x_prompt, x_sample, state_ret, cache_k, cache_v, page_table, g_ffn1, w_ffn1_gu, w_ffn1_down, g_mix, g_ffn2, w_ffn2_gu, w_ffn2_down, w_ret_qkvg, g_ret_gn, w_ret_o, g_kv, w_kv, g_k_norm, w_diff_q, g_q_norm, lambda_q1, lambda_k1, lambda_q2, lambda_k2, g_sub, w_diff_o):
    B, S, D = x_prompt.shape
    Bd, Sd, _ = x_sample.shape
    pos_p = jnp.arange(S)
    pos_s = jnp.broadcast_to(PAST_LEN + jnp.arange(Sd), (Bd * Sd,))
    xp = x_prompt.reshape(B * S, D)
    xs = x_sample.reshape(Bd * Sd, D)
    ret_p, ret_s = [], []
    kp32 = vp32 = ks32 = vs32 = kp = vp = None
    for i in range(DEPTH):
        if i == N_A_LAYERS:
            kp32, kp, vp32, vp = shared_kv(xp, g_kv, w_kv, g_k_norm, pos_p, BF16)
            ks32, _, vs32, _ = shared_kv(xs, g_kv, w_kv, g_k_norm, pos_s, F32)
        xp = swiglu_half_step(xp, g_ffn1[i], w_ffn1_gu[i], w_ffn1_down[i], BF16)
        xs = swiglu_half_step(xs, g_ffn1[i], w_ffn1_gu[i], w_ffn1_down[i], F32)
        if i < N_A_LAYERS:
            a = i
            qk, v, g = ret_project(rmsnorm_cast(xp, g_mix[i], BF16), w_ret_qkvg[a], pos_p, BF16)
            y, s_fin = retention_prompt(qk, v, g, g_ret_gn[a], B, S)
            xp = matmul_residual(y, w_ret_o[a], xp, 1.0, name="ret_out_proj")
            ret_p.append(s_fin)
            qk, v, g = ret_project(rmsnorm_cast(xs, g_mix[i], F32), w_ret_qkvg[a], pos_s, F32)
            y, s_new = retention_step(qk, v, g, g_ret_gn[a], state_ret[a])
            xs = matmul_residual(y, w_ret_o[a], xs, 1.0, name="ret_out_proj")
            ret_s.append(s_new)
        else:
            bidx = i - N_A_LAYERS
            lam_init = 0.8 - 0.6 * math.exp(-0.3 * i)
            lam_rows = jnp.stack([lambda_q1[bidx], lambda_k1[bidx],
                                  lambda_q2[bidx], lambda_k2[bidx]]).astype(F32)
            q = diff_query(rmsnorm_cast(xp, g_mix[i], BF16), w_diff_q[bidx], g_q_norm[bidx], pos_p, BF16)
            o = diff_attn_prompt(q, kp, vp, lam_rows, g_sub[bidx], lam_init, B, S)
            xp = matmul_residual(o, w_diff_o[bidx], xp, 1.0, name="diff_out_proj")
            q = diff_query(rmsnorm_cast(xs, g_mix[i], F32), w_diff_q[bidx], g_q_norm[bidx], pos_s, F32)
            o = diff_attn_decode(q, ks32, vs32, cache_k, cache_v, page_table, lam_rows, g_sub[bidx], lam_init)
            xs = matmul_residual(o, w_diff_o[bidx], xs, 1.0, name="diff_out_proj")
        xp = swiglu_half_step(xp, g_ffn2[i], w_ffn2_gu[i], w_ffn2_down[i], BF16)
        xs = swiglu_half_step(xs, g_ffn2[i], w_ffn2_gu[i], w_ffn2_down[i], F32)
    return (xp.reshape(B, S, D), xs.reshape(Bd, Sd, D),
            jnp.stack(ret_p), jnp.stack(ret_s),
            kp32.reshape(B, S, DIFF_HEADS, 2 * DIFF_DH), vp32.reshape(B, S, DIFF_HEADS, DIFF_DV),
            ks32.reshape(Bd, Sd, DIFF_HEADS, 2 * DIFF_DH), vs32.reshape(Bd, Sd, DIFF_HEADS, DIFF_DV))
```

```python
import functools
import math

import jax
import jax.numpy as jnp
from jax import lax
from jax.experimental import pallas as pl
from jax.experimental.pallas import tpu as pltpu

F32 = jnp.float32
BF16 = jnp.bfloat16

D_MODEL = 4096
SEQ = 2048
DEPTH = 2
PAST_LEN = 16384
PAGE_SIZE = 128
N_A_LAYERS = DEPTH // 2
RET_HEADS = 16
RET_DK = D_MODEL // RET_HEADS
RET_CHUNK = 128
RET_THETA = 10000.0
DIFF_HEADS = 16
DIFF_DH = D_MODEL // (2 * DIFF_HEADS)
DIFF_DV = 2 * DIFF_DH
ROT_DIM = DIFF_DH // 4
ROPE_THETA = 500000.0
D_FF = 256 * ((8 * D_MODEL // 3 + 255) // 256)
EPS = 1e-6

LANES = 128
VMEM_LIMIT_BYTES = 56 * 1024 * 1024
CAST_ROWS = 128


def _params(n_grid_dims):
    return pltpu.CompilerParams(
        dimension_semantics=("arbitrary",) * n_grid_dims,
        vmem_limit_bytes=VMEM_LIMIT_BYTES)


def _silu(x):
    return x * jax.nn.sigmoid(x)


def _rmsnorm_body(x_ref, g_ref, o_ref):
    x = x_ref[...].astype(F32)
    y = x * lax.rsqrt(jnp.mean(x * x, axis=-1, keepdims=True) + EPS)
    o_ref[...] = (y * g_ref[...]).astype(o_ref.dtype)


def rmsnorm_cast(x, g, out_dtype):
    M, D = x.shape
    tm = min(M, 512)
    return pl.pallas_call(
        _rmsnorm_body,
        grid=(M // tm,),
        in_specs=[pl.BlockSpec((tm, D), lambda i: (i, 0)),
                  pl.BlockSpec((1, D), lambda i: (0, 0))],
        out_specs=pl.BlockSpec((tm, D), lambda i: (i, 0)),
        out_shape=jax.ShapeDtypeStruct((M, D), out_dtype),
        compiler_params=_params(1),
        name="rmsnorm_cast",
    )(x, g.reshape(1, D).astype(F32))


def _ws_matmul_body(*refs, n_w, n_extra, n_out, tnw, tk, epilogue):
    x_ref = refs[0]
    w_refs = refs[1:1 + n_w]
    extra_refs = refs[1 + n_w:1 + n_w + n_extra]
    out_refs = refs[1 + n_w + n_extra:1 + n_w + n_extra + n_out]
    wbf_ref = refs[-1]

    @pl.when(pl.program_id(1) == 0)
    def _():
        rows = min(CAST_ROWS, tk)
        assert tk % rows == 0

        def cast_rows(r, carry):
            sl = pl.ds(pl.multiple_of(r * rows, rows), rows)
            for t, w_ref in enumerate(w_refs):
                wbf_ref[sl, t * tnw:(t + 1) * tnw] = w_ref[sl, :].astype(BF16)
            return carry

        lax.fori_loop(0, tk // rows, cast_rows, 0)

    acc = jnp.dot(x_ref[...].astype(BF16), wbf_ref[...], preferred_element_type=F32)
    epilogue(acc, extra_refs, out_refs)


def ws_matmul(x, w, col_block_offsets, n_tiles, tm, tnw, epilogue, outs,
              extras=(), k_block=0, tk=None, name="ws_matmul"):
    M = x.shape[0]
    w_all, layer = w
    tk = x.shape[1] if tk is None else tk
    n_w = len(col_block_offsets)
    in_specs = [pl.BlockSpec((tm, tk), lambda j, i: (i, k_block))]
    for off in col_block_offsets:
        in_specs.append(pl.BlockSpec((None, tk, tnw), lambda j, i, off=off: (layer, k_block, off + j)))
    for _, bshape, imap in extras:
        in_specs.append(pl.BlockSpec(bshape, imap))
    body = functools.partial(_ws_matmul_body, n_w=n_w, n_extra=len(extras), n_out=len(outs),
                             tnw=tnw, tk=tk, epilogue=epilogue)
    res = pl.pallas_call(
        body,
        grid=(n_tiles, M // tm),
        in_specs=in_specs,
        out_specs=[pl.BlockSpec(bshape, imap) for _, bshape, imap in outs],
        out_shape=[s for s, _, _ in outs],
        scratch_shapes=[pltpu.VMEM((tk, n_w * tnw), BF16)],
        compiler_params=_params(2),
        name=name,
    )(x, *([w_all] * n_w), *[a for a, _, _ in extras])
    return res


def _row_tile(M):
    return min(M, 1024)


def ffn_gate_up(h, w_gu, out_dtype):
    M = h.shape[0]
    d_ff = w_gu[0].shape[2] // 2
    tnw = 256
    tm = _row_tile(M)
    n_tiles = d_ff // tnw

    def epilogue(acc, extra_refs, out_refs):
        gate, up = acc[:, :tnw], acc[:, tnw:]
        out_refs[0][...] = (_silu(gate) * up).astype(out_refs[0].dtype)

    out = (jax.ShapeDtypeStruct((M, d_ff), out_dtype), (tm, tnw), lambda j, i: (i, j))
    return ws_matmul(h, w_gu, (0, n_tiles), n_tiles, tm, tnw, epilogue, [out],
                     name="ffn_gate_up")[0]


def matmul_residual(a, w, res, alpha, k_splits=1, name="matmul_residual"):
    M, K = a.shape
    N = w[0].shape[2]
    tm = min(M, 512)
    tn = 512
    tk = K // k_splits

    def epilogue(acc, extra_refs, out_refs):
        out_refs[0][...] = extra_refs[0][...] + alpha * acc

    out = (jax.ShapeDtypeStruct((M, N), F32), (tm, tn), lambda j, i: (i, j))
    for kb in range(k_splits):
        extra = (res, (tm, tn), lambda j, i: (i, j))
        res = ws_matmul(a, w, (0,), N // tn, tm, tn, epilogue, [out], extras=[extra],
                        k_block=kb, tk=tk, name=name)[0]
    return res


def swiglu_half_step(x, g, w_gu, w_down, act_dtype):
    h = rmsnorm_cast(x, g, act_dtype)
    u = ffn_gate_up(h, w_gu, act_dtype)
    return matmul_residual(u, w_down, x, 0.5, k_splits=2, name="ffn_down")


def _rope_tables_full(pos, rot_dim, theta):
    half = rot_dim // 2
    inv = 1.0 / (theta ** (jnp.arange(half, dtype=F32) / half))
    ang = pos.astype(F32)[:, None] * inv[None, :]
    return jnp.cos(ang), jnp.sin(ang)


def _rope_tables_partial(pos, rot_dim, theta, width):
    half = rot_dim // 2
    cos, sin = _rope_tables_full(pos, rot_dim, theta)
    P = pos.shape[0]
    ones = jnp.ones((P, width - rot_dim), F32)
    zeros_rest = jnp.zeros((P, width - rot_dim), F32)
    zeros_half = jnp.zeros((P, half), F32)
    c = jnp.concatenate([cos, cos, ones], axis=1)
    s_lo = jnp.concatenate([-sin, zeros_half, zeros_rest], axis=1)
    s_hi = jnp.concatenate([zeros_half, sin, zeros_rest], axis=1)
    return c, s_lo, s_hi


def ret_project(h, w_qkvg, pos, act_dtype):
    M, D = h.shape
    tm = _row_tile(M)
    tn = 2 * RET_DK
    half = RET_DK // 2
    cos, sin = _rope_tables_full(pos, RET_DK, RET_THETA)
    P = cos.shape[0]
    n_pos_blocks = max(P // tm, 1)
    k_scale = RET_DK ** -0.5
    n_head_tiles = D // tn

    def rope_epilogue(acc, extra_refs, out_refs):
        c = extra_refs[0][...]
        s = extra_refs[1][...]
        is_k = pl.program_id(0) >= n_head_tiles
        scale = jnp.where(is_k, k_scale, 1.0).astype(F32)
        for hh in range(tn // RET_DK):
            x1 = acc[:, hh * RET_DK:hh * RET_DK + half]
            x2 = acc[:, hh * RET_DK + half:(hh + 1) * RET_DK]
            out_refs[0][:, hh * RET_DK:hh * RET_DK + half] = (
                (x1 * c - x2 * s) * scale).astype(out_refs[0].dtype)
            out_refs[0][:, hh * RET_DK + half:(hh + 1) * RET_DK] = (
                (x2 * c + x1 * s) * scale).astype(out_refs[0].dtype)

    def plain_epilogue(acc, extra_refs, out_refs):
        out_refs[0][...] = acc.astype(out_refs[0].dtype)

    tab = lambda a: (a, (tm, half), lambda j, i: (i % n_pos_blocks, 0))
    qk = ws_matmul(h, w_qkvg, (0,), 2 * n_head_tiles, tm, tn, rope_epilogue,
                   [(jax.ShapeDtypeStruct((M, 2 * D), act_dtype), (tm, tn), lambda j, i: (i, j))],
                   extras=[tab(cos), tab(sin)], name="ret_qk_proj")[0]
    v = ws_matmul(h, w_qkvg, (2 * n_head_tiles,), n_head_tiles, tm, tn, plain_epilogue,
                  [(jax.ShapeDtypeStruct((M, D), act_dtype), (tm, tn), lambda j, i: (i, j))],
                  name="ret_v_proj")[0]
    g = ws_matmul(h, w_qkvg, (3 * n_head_tiles,), n_head_tiles, tm, tn, plain_epilogue,
                  [(jax.ShapeDtypeStruct((M, D), F32), (tm, tn), lambda j, i: (i, j))],
                  name="ret_g_proj")[0]
    return qk, v, g


def _log_gamma_rows():
    lg = jnp.log1p(-(2.0 ** (-5.0 - jnp.arange(RET_HEADS, dtype=F32))))
    return jnp.broadcast_to(lg[:, None, None], (RET_HEADS, 1, RET_DK))


def _group_norm_gate(o, gn, g):
    mu = jnp.mean(o, axis=-1, keepdims=True)
    d = o - mu
    var = jnp.mean(d * d, axis=-1, keepdims=True)
    y = d * lax.rsqrt(var + EPS) * gn
    return _silu(g) * y


def _ret_prompt_body(q_ref, k_ref, v_ref, g_ref, lg_ref, gn_ref, y_ref, s_ref, state_ref, *, n_chunks):
    L = RET_CHUNK
    lg = lg_ref[0]
    r = lax.broadcasted_iota(jnp.int32, (L, L), 0)
    c = lax.broadcasted_iota(jnp.int32, (L, L), 1)
    diff = (r - c).astype(F32)
    dmat = jnp.where(diff >= 0, jnp.exp(jnp.maximum(diff, 0.0) * lg[:, :L]), 0.0)
    rows = lax.broadcasted_iota(jnp.int32, (L, RET_DK), 0).astype(F32)
    q_decay = jnp.exp((rows + 1.0) * lg)
    k_decay = jnp.exp((L - 1.0 - rows) * lg)
    chunk_decay = jnp.exp(float(L) * lg)
    gn = gn_ref[...]
    state_ref[...] = jnp.zeros_like(state_ref)

    def chunk(ci, carry):
        sl = pl.ds(pl.multiple_of(ci * L, L), L)
        q = q_ref[sl, :]
        k = k_ref[sl, :]
        v = v_ref[sl, :]
        state = state_ref[...]
        scores = lax.dot_general(q, k, (((1,), (1,)), ((), ())), preferred_element_type=F32) * dmat
        inner = jnp.dot(scores.astype(BF16), v, preferred_element_type=F32)
        cross = jnp.dot(q, state.astype(BF16), preferred_element_type=F32) * q_decay
        o = inner + cross
        kd = (k.astype(F32) * k_decay).astype(BF16)
        state_ref[...] = chunk_decay * state + lax.dot_general(
            kd, v, (((0,), (0,)), ((), ())), preferred_element_type=F32)
        y_ref[sl, :] = _group_norm_gate(o, gn, g_ref[sl, :]).astype(y_ref.dtype)
        return carry

    lax.fori_loop(0, n_chunks, chunk, 0)
    s_ref[0, 0] = state_ref[...]


def retention_prompt(qk, v, g, gn_gain, batch, seq):
    M, D = v.shape
    H, DK = RET_HEADS, RET_DK
    body = functools.partial(_ret_prompt_body, n_chunks=seq // RET_CHUNK)
    y, s_fin = pl.pallas_call(
        body,
        grid=(batch, H),
        in_specs=[pl.BlockSpec((seq, DK), lambda b, h: (b, h)),
                  pl.BlockSpec((seq, DK), lambda b, h: (b, H + h)),
                  pl.BlockSpec((seq, DK), lambda b, h: (b, h)),
                  pl.BlockSpec((seq, DK), lambda b, h: (b, h)),
                  pl.BlockSpec((1, 1, DK), lambda b, h: (h, 0, 0)),
                  pl.BlockSpec((1, DK), lambda b, h: (0, h))],
        out_specs=[pl.BlockSpec((seq, DK), lambda b, h: (b, h)),
                   pl.BlockSpec((1, 1, DK, DK), lambda b, h: (b, h, 0, 0))],
        out_shape=[jax.ShapeDtypeStruct((M, D), BF16),
                   jax.ShapeDtypeStruct((batch, H, DK, DK), F32)],
        scratch_shapes=[pltpu.VMEM((DK, DK), F32)],
        compiler_params=_params(2),
        name="retention_prompt",
    )(qk, qk, v, g, _log_gamma_rows(), gn_gain.reshape(1, D).astype(F32))
    return y, s_fin


def _ret_step_body(q_ref, k_ref, v_ref, g_ref, lg_ref, gn_ref, s_in_ref, y_ref, s_out_ref, *, batch):
    lg = lg_ref[0]
    d0 = jnp.exp(0.0 * lg)
    q_decay = jnp.exp(1.0 * lg)
    k_decay = jnp.exp(0.0 * lg)
    step_decay = jnp.exp(1.0 * lg)
    rnd = lambda t: t.astype(BF16).astype(F32)
    qf, kf, vf = rnd(q_ref[...]), rnd(k_ref[...]), rnd(v_ref[...])
    row = lax.broadcasted_iota(jnp.int32, qf.shape, 0)
    scores = jnp.sum(qf * kf, axis=-1, keepdims=True) * d0[:, :1]
    inner = rnd(scores) * vf
    kd = rnd(kf * k_decay)
    cross = jnp.zeros(qf.shape, F32)
    for b in range(batch):
        state = s_in_ref[b, 0].astype(F32)
        q_only_b = jnp.where(row == b, qf, 0.0)
        k_only_b = jnp.where(row == b, kd, 0.0)
        cross = cross + jnp.dot(q_only_b, rnd(state), preferred_element_type=F32)
        s_out_ref[b, 0] = step_decay * state + lax.dot_general(
            k_only_b, vf, (((0,), (0,)), ((), ())), preferred_element_type=F32)
    o = inner + cross * q_decay
    y_ref[...] = _group_norm_gate(o, gn_ref[...], g_ref[...]).astype(y_ref.dtype)


def retention_step(qk, v, g, gn_gain, state):
    B, D = v.shape
    H, DK = RET_HEADS, RET_DK
    body = functools.partial(_ret_step_body, batch=B)
    y, s_new = pl.pallas_call(
        body,
        grid=(H,),
        in_specs=[pl.BlockSpec((B, DK), lambda h: (0, h)),
                  pl.BlockSpec((B, DK), lambda h: (0, H + h)),
                  pl.BlockSpec((B, DK), lambda h: (0, h)),
                  pl.BlockSpec((B, DK), lambda h: (0, h)),
                  pl.BlockSpec((1, 1, DK), lambda h: (h, 0, 0)),
                  pl.BlockSpec((1, DK), lambda h: (0, h)),
                  pl.BlockSpec((B, 1, DK, DK), lambda h: (0, h, 0, 0))],
        out_specs=[pl.BlockSpec((B, DK), lambda h: (0, h)),
                   pl.BlockSpec((B, 1, DK, DK), lambda h: (0, h, 0, 0))],
        out_shape=[jax.ShapeDtypeStruct((B, D), F32),
                   jax.ShapeDtypeStruct((B, H, DK, DK), F32)],
        compiler_params=_params(1),
        name="retention_step",
    )(qk, qk, v, g, _log_gamma_rows(), gn_gain.reshape(1, D).astype(F32), state)
    return y, s_new


def _norm_rope_groups(acc, gain, c, s_lo, s_hi, n_groups):
    half = ROT_DIM // 2
    outs = []
    for gi in range(n_groups):
        xg = acc[:, gi * DIFF_DH:(gi + 1) * DIFF_DH]
        yg = xg * lax.rsqrt(jnp.mean(xg * xg, axis=-1, keepdims=True) + EPS) * gain
        rot = (yg * c + pltpu.roll(yg, DIFF_DH - half, 1) * s_lo + pltpu.roll(yg, half, 1) * s_hi)
        outs.append(rot)
    return outs


def _diff_rope_extras(pos, tm):
    c, s_lo, s_hi = _rope_tables_partial(pos, ROT_DIM, ROPE_THETA, DIFF_DH)
    n_pos_blocks = max(pos.shape[0] // tm, 1)
    return [(a, (tm, DIFF_DH), lambda j, i: (i % n_pos_blocks, 0)) for a in (c, s_lo, s_hi)]


def shared_kv(x, g_kv, w_kv, g_k_norm, pos, act_dtype):
    M, D = x.shape
    h = rmsnorm_cast(x, g_kv, act_dtype)
    tm = _row_tile(M)
    tn = 4 * DIFF_DH
    n_groups = tn // DIFF_DH
    nk = (DIFF_HEADS * 2 * DIFF_DH) // tn
    nv = (DIFF_HEADS * DIFF_DV) // tn

    def k_epilogue(acc, extra_refs, out_refs):
        gain, c, s_lo, s_hi = (r[...] for r in extra_refs)
        for gi, rot in enumerate(_norm_rope_groups(acc, gain, c, s_lo, s_hi, n_groups)):
            out_refs[0][:, gi * DIFF_DH:(gi + 1) * DIFF_DH] = rot
            out_refs[1][:, gi * DIFF_DH:(gi + 1) * DIFF_DH] = rot.astype(out_refs[1].dtype)

    def v_epilogue(acc, extra_refs, out_refs):
        out_refs[0][...] = acc
        out_refs[1][...] = acc.astype(out_refs[1].dtype)

    gain = (g_k_norm.reshape(1, DIFF_DH).astype(F32), (1, DIFF_DH), lambda j, i: (0, 0))
    two_outs = lambda n: [(jax.ShapeDtypeStruct((M, n), F32), (tm, tn), lambda j, i: (i, j)),
                          (jax.ShapeDtypeStruct((M, n), act_dtype), (tm, tn), lambda j, i: (i, j))]
    k32, kact = ws_matmul(h, w_kv, (0,), nk, tm, tn, k_epilogue, two_outs(nk * tn),
                          extras=[gain] + _diff_rope_extras(pos, tm), name="shared_k_proj")
    v32, vact = ws_matmul(h, w_kv, (nk,), nv, tm, tn, v_epilogue, two_outs(nv * tn),
                          name="shared_v_proj")
    return k32, kact, v32, vact


def diff_query(h, w_q, g_q_norm, pos, act_dtype):
    M, D = h.shape
    tm = _row_tile(M)
    tn = 4 * DIFF_DH
    n_groups = tn // DIFF_DH
    q_scale = DIFF_DH ** -0.5

    def q_epilogue(acc, extra_refs, out_refs):
        gain, c, s_lo, s_hi = (r[...] for r in extra_refs)
        for gi, rot in enumerate(_norm_rope_groups(acc, gain, c, s_lo, s_hi, n_groups)):
            out_refs[0][:, gi * DIFF_DH:(gi + 1) * DIFF_DH] = (rot * q_scale).astype(out_refs[0].dtype)

    gain = (g_q_norm.reshape(1, DIFF_DH).astype(F32), (1, DIFF_DH), lambda j, i: (0, 0))
    N = w_q[0].shape[2]
    return ws_matmul(h, w_q, (0,), N // tn, tm, tn, q_epilogue,
                     [(jax.ShapeDtypeStruct((M, N), act_dtype), (tm, tn), lambda j, i: (i, j))],
                     extras=[gain] + _diff_rope_extras(pos, tm), name="diff_q_proj")[0]


def _lambda_value(lam_ref, lam_init):
    l = lam_ref[...]
    a = jnp.sum(l[0:1, :] * l[1:2, :], axis=-1, keepdims=True)
    b = jnp.sum(l[2:3, :] * l[3:4, :], axis=-1, keepdims=True)
    return jnp.exp(a) - jnp.exp(b) + lam_init


def _sub_norm(o, g_sub, lam_init):
    y = o * lax.rsqrt(jnp.mean(o * o, axis=-1, keepdims=True) + EPS) * g_sub
    return y * (1.0 - lam_init)


def _diff_attn_body(q_ref, k_ref, v_ref, lam_ref, gsub_ref, o_ref, m_ref, l_ref, acc_ref, *, tq, lam_init):
    qi = pl.program_id(2)
    tk = tq
    DH, DV = DIFF_DH, DIFF_DV
    m_ref[...] = jnp.full(m_ref.shape, -jnp.inf, F32)
    l_ref[...] = jnp.zeros(l_ref.shape, F32)
    acc_ref[...] = jnp.zeros(acc_ref.shape, F32)

    def update(j, masked):
        ksl = pl.ds(pl.multiple_of(j * tk, tk), tk)
        kblk = k_ref[ksl, :]
        vblk = v_ref[ksl, :]
        for c in range(2):
            s = lax.dot_general(q_ref[:, c * DH:(c + 1) * DH], kblk[:, c * DH:(c + 1) * DH],
                                (((1,), (1,)), ((), ())), preferred_element_type=F32)
            if masked:
                rr = lax.broadcasted_iota(jnp.int32, (tq, tk), 0)
                cc = lax.broadcasted_iota(jnp.int32, (tq, tk), 1)
                s = jnp.where(cc <= rr, s, -jnp.inf)
            m_prev = m_ref[c]
            m_new = jnp.maximum(m_prev, jnp.max(s, axis=-1, keepdims=True))
            corr = jnp.exp(m_prev - m_new)
            p = jnp.exp(s - m_new[:, :1])
            l_ref[c] = l_ref[c] * corr + jnp.sum(p, axis=-1, keepdims=True)
            acc_ref[c] = acc_ref[c] * corr[:, :1] + jnp.dot(p.astype(BF16), vblk, preferred_element_type=F32)
            m_ref[c] = m_new

    def body(j, carry):
        update(j, False)
        return carry

    lax.fori_loop(0, qi, body, 0)
    update(qi, True)

    lam = _lambda_value(lam_ref, lam_init)
    o = acc_ref[0] / l_ref[0][:, :1] - lam * (acc_ref[1] / l_ref[1][:, :1])
    o_ref[...] = _sub_norm(o, gsub_ref[...], lam_init).astype(o_ref.dtype)


def diff_attn_prompt(q, k, v, lam_rows, g_sub, lam_init, batch, seq, tq=512):
    M, D = q.shape
    H, DV = DIFF_HEADS, DIFF_DV
    tq = min(seq, tq)
    nq = seq // tq
    body = functools.partial(_diff_attn_body, tq=tq, lam_init=lam_init)
    return pl.pallas_call(
        body,
        grid=(batch, H, nq),
        in_specs=[pl.BlockSpec((tq, DV), lambda b, h, i: (b * nq + i, h)),
                  pl.BlockSpec((seq, DV), lambda b, h, i: (b, h)),
                  pl.BlockSpec((seq, DV), lambda b, h, i: (b, h)),
                  pl.BlockSpec((4, DIFF_DH), lambda b, h, i: (0, 0)),
                  pl.BlockSpec((1, DV), lambda b, h, i: (0, 0))],
        out_specs=pl.BlockSpec((tq, DV), lambda b, h, i: (b * nq + i, h)),
        out_shape=jax.ShapeDtypeStruct((M, D), BF16),
        scratch_shapes=[pltpu.VMEM((2, tq, LANES), F32),
                        pltpu.VMEM((2, tq, LANES), F32),
                        pltpu.VMEM((2, tq, DV), F32)],
        compiler_params=_params(3),
        name="diff_attn_prompt",
    )(q, k, v, lam_rows, g_sub.reshape(1, DV).astype(F32))


def _diff_decode_body(pt_ref, qt_ref, *refs, pages_per_step, lam_init):
    k_refs = refs[:pages_per_step]
    v_refs = refs[pages_per_step:2 * pages_per_step]
    knew_ref, vnew_ref, lam_ref, gsub_ref, o_ref, m_ref, l_ref, acc_ref = refs[2 * pages_per_step:]
    step = pl.program_id(1)
    n_steps = pl.num_programs(1)
    H = DIFF_HEADS
    NQ = 2 * H

    @pl.when(step == 0)
    def _():
        m_ref[...] = jnp.full(m_ref.shape, -jnp.inf, F32)
        l_ref[...] = jnp.zeros(l_ref.shape, F32)
        acc_ref[...] = jnp.zeros(acc_ref.shape, F32)

    qt = qt_ref[0]

    def update(kb, vb, n_real_rows):
        n_rows = kb.shape[0]
        st = lax.dot_general(qt, kb, (((1,), (1,)), ((), ())), preferred_element_type=F32)
        rr = lax.broadcasted_iota(jnp.int32, (NQ, n_rows), 0)
        cc = lax.broadcasted_iota(jnp.int32, (NQ, n_rows), 1)
        valid = (cc % H) == (rr % H)
        if n_real_rows < n_rows:
            valid = valid & (cc < n_real_rows)
        st = jnp.where(valid, st, -jnp.inf)
        m_prev = m_ref[...]
        m_new = jnp.maximum(m_prev, jnp.max(st, axis=-1, keepdims=True))
        corr = jnp.exp(m_prev - m_new)
        p = jnp.exp(st - m_new[:, :1])
        l_ref[...] = l_ref[...] * corr + jnp.sum(p, axis=-1, keepdims=True)
        acc_ref[...] = acc_ref[...] * corr[:, :1] + jnp.dot(p.astype(BF16), vb, preferred_element_type=F32)
        m_ref[...] = m_new

    for r in range(pages_per_step):
        update(k_refs[r][0].astype(BF16), v_refs[r][0].astype(BF16), PAGE_SIZE * H)

    @pl.when(step == n_steps - 1)
    def _():
        pad = jnp.zeros((LANES - H, knew_ref.shape[2]), F32)
        kn = jnp.concatenate([knew_ref[0], pad], axis=0).astype(BF16)
        vn = jnp.concatenate([vnew_ref[0], pad], axis=0).astype(BF16)
        update(kn, vn, H)
        lam = _lambda_value(lam_ref, lam_init)
        acc = acc_ref[...]
        l = l_ref[...]
        o = acc[0:H] / l[0:H, :1] - lam * (acc[H:NQ] / l[H:NQ, :1])
        o_ref[0] = _sub_norm(o, gsub_ref[...], lam_init)


def diff_attn_decode(q, k_new, v_new, cache_k, cache_v, page_table, lam_rows, g_sub, lam_init):
    B, D = q.shape
    H, DH, DV = DIFF_HEADS, DIFF_DH, DIFF_DV
    n_pool = cache_k.shape[0]
    n_pages = page_table.shape[1]
    pages_per_step = 2
    n_steps = n_pages // pages_per_step
    rows = PAGE_SIZE * H
    ck = cache_k.reshape(n_pool, rows, 2 * DH)
    cv = cache_v.reshape(n_pool, rows, DV)
    q4 = q.reshape(B, H, 2, DH)
    zero = jnp.zeros((B, H, DH), q.dtype)
    qt = jnp.concatenate([jnp.concatenate([q4[:, :, 0], zero], axis=-1),
                          jnp.concatenate([zero, q4[:, :, 1]], axis=-1)], axis=1).astype(BF16)

    def page_spec(r, width):
        return pl.BlockSpec((1, rows, width), lambda b, s, pt: (pt[b, s * pages_per_step + r], 0, 0))

    body = functools.partial(_diff_decode_body, pages_per_step=pages_per_step, lam_init=lam_init)
    grid_spec = pltpu.PrefetchScalarGridSpec(
        num_scalar_prefetch=1,
        grid=(B, n_steps),
        in_specs=[pl.BlockSpec((1, 2 * H, 2 * DH), lambda b, s, pt: (b, 0, 0))]
                 + [page_spec(r, 2 * DH) for r in range(pages_per_step)]
                 + [page_spec(r, DV) for r in range(pages_per_step)]
                 + [pl.BlockSpec((1, H, 2 * DH), lambda b, s, pt: (b, 0, 0)),
                    pl.BlockSpec((1, H, DV), lambda b, s, pt: (b, 0, 0)),
                    pl.BlockSpec((4, DH), lambda b, s, pt: (0, 0)),
                    pl.BlockSpec((1, DV), lambda b, s, pt: (0, 0))],
        out_specs=pl.BlockSpec((1, H, DV), lambda b, s, pt: (b, 0, 0)),
        scratch_shapes=[pltpu.VMEM((2 * H, LANES), F32),
                        pltpu.VMEM((2 * H, LANES), F32),
                        pltpu.VMEM((2 * H, DV), F32)],
    )
    out = pl.pallas_call(
        body,
        grid_spec=grid_spec,
        out_shape=jax.ShapeDtypeStruct((B, H, DV), F32),
        compiler_params=_params(2),
        name="diff_attn_decode",
    )(page_table, qt, *([ck] * pages_per_step), *([cv] * pages_per_step),
      k_new.reshape(B, H, 2 * DH), v_new.reshape(B, H, DV), lam_rows, g_sub.reshape(1, DV).astype(F32))
    return out.reshape(B, H * DV)


def kernel(x_prompt, x_sample, state_ret, cache_k, cache_v, page_table, g_ffn1, w_ffn1_gu, w_ffn1_down, g_mix, g_ffn2, w_ffn2_gu, w_ffn2_down, w_ret_qkvg, g_ret_gn, w_ret_o, g_kv, w_kv, g_k_norm, w_diff_q, g_q_norm, lambda_q1, lambda_k1, lambda_q2, lambda_k2, g_sub, w_diff_o):
    B, S, D = x_prompt.shape
    Bd, Sd, _ = x_sample.shape
    pos_p = jnp.arange(S)
    pos_s = jnp.broadcast_to(PAST_LEN + jnp.arange(Sd), (Bd * Sd,))
    xp = x_prompt.reshape(B * S, D)
    xs = x_sample.reshape(Bd * Sd, D)
    ret_p, ret_s = [], []
    kp32 = vp32 = ks32 = vs32 = kp = vp = None
    for i in range(DEPTH):
        if i == N_A_LAYERS:
            w_kv1 = (w_kv[None], 0)
            kp32, kp, vp32, vp = shared_kv(xp, g_kv, w_kv1, g_k_norm, pos_p, BF16)
            ks32, _, vs32, _ = shared_kv(xs, g_kv, w_kv1, g_k_norm, pos_s, F32)
        xp = swiglu_half_step(xp, g_ffn1[i], (w_ffn1_gu, i), (w_ffn1_down, i), BF16)
        xs = swiglu_half_step(xs, g_ffn1[i], (w_ffn1_gu, i), (w_ffn1_down, i), F32)
        if i < N_A_LAYERS:
            a = i
            qk, v, g = ret_project(rmsnorm_cast(xp, g_mix[i], BF16), (w_ret_qkvg, a), pos_p, BF16)
            y, s_fin = retention_prompt(qk, v, g, g_ret_gn[a], B, S)
            xp = matmul_residual(y, (w_ret_o, a), xp, 1.0, name="ret_out_proj")
            ret_p.append(s_fin)
            qk, v, g = ret_project(rmsnorm_cast(xs, g_mix[i], F32), (w_ret_qkvg, a), pos_s, F32)
            y, s_new = retention_step(qk, v, g, g_ret_gn[a], state_ret[a])
            xs = matmul_residual(y, (w_ret_o, a), xs, 1.0, name="ret_out_proj")
            ret_s.append(s_new)
        else:
            bidx = i - N_A_LAYERS
            lam_init = 0.8 - 0.6 * math.exp(-0.3 * i)
            lam_rows = jnp.stack([lambda_q1[bidx], lambda_k1[bidx],
                                  lambda_q2[bidx], lambda_k2[bidx]]).astype(F32)
            q = diff_query(rmsnorm_cast(xp, g_mix[i], BF16), (w_diff_q, bidx), g_q_norm[bidx], pos_p, BF16)
            o = diff_attn_prompt(q, kp, vp, lam_rows, g_sub[bidx], lam_init, B, S)
            xp = matmul_residual(o, (w_diff_o, bidx), xp, 1.0, name="diff_out_proj")
            q = diff_query(rmsnorm_cast(xs, g_mix[i], F32), (w_diff_q, bidx), g_q_norm[bidx], pos_s, F32)
            o = diff_attn_decode(q, ks32, vs32, cache_k, cache_v, page_table, lam_rows, g_sub[bidx], lam_init)
            xs = matmul_residual(o, (w_diff_o, bidx), xs, 1.0, name="diff_out_proj")
        xp = swiglu_half_step(xp, g_ffn2[i], (w_ffn2_gu, i), (w_ffn2_down, i), BF16)
        xs = swiglu_half_step(xs, g_ffn2[i], (w_ffn2_gu, i), (w_ffn2_down, i), F32)
    return (xp.reshape(B, S, D), xs.reshape(Bd, Sd, D),
            jnp.stack(ret_p), jnp.stack(ret_s),
            kp32.reshape(B, S, DIFF_HEADS, 2 * DIFF_DH), vp32.reshape(B, S, DIFF_HEADS, DIFF_DV),
            ks32.reshape(Bd, Sd, DIFF_HEADS, 2 * DIFF_DH), vs32.reshape(Bd, Sd, DIFF_HEADS, DIFF_DV))
```

```python
import functools
import math

import jax
import jax.numpy as jnp
from jax import lax
from jax.experimental import pallas as pl
from jax.experimental.pallas import tpu as pltpu

F32 = jnp.float32
BF16 = jnp.bfloat16

D_MODEL = 4096
SEQ = 2048
DEPTH = 2
PAST_LEN = 16384
PAGE_SIZE = 128
N_A_LAYERS = DEPTH // 2
RET_HEADS = 16
RET_DK = D_MODEL // RET_HEADS
RET_CHUNK = 128
RET_THETA = 10000.0
DIFF_HEADS = 16
DIFF_DH = D_MODEL // (2 * DIFF_HEADS)
DIFF_DV = 2 * DIFF_DH
ROT_DIM = DIFF_DH // 4
ROPE_THETA = 500000.0
D_FF = 256 * ((8 * D_MODEL // 3 + 255) // 256)
EPS = 1e-6

LANES = 128
VMEM_LIMIT_BYTES = 56 * 1024 * 1024
CAST_ROWS = 128


def _params(n_grid_dims):
    return pltpu.CompilerParams(
        dimension_semantics=("arbitrary",) * n_grid_dims,
        vmem_limit_bytes=VMEM_LIMIT_BYTES)


def _silu(x):
    return x * jax.nn.sigmoid(x)


def _rmsnorm_body(x_ref, g_ref, o_ref):
    x = x_ref[...].astype(F32)
    y = x * lax.rsqrt(jnp.mean(x * x, axis=-1, keepdims=True) + EPS)
    o_ref[...] = (y * g_ref[...]).astype(o_ref.dtype)


def rmsnorm_cast(x, g, out_dtype):
    M, D = x.shape
    tm = min(M, 512)
    return pl.pallas_call(
        _rmsnorm_body,
        grid=(M // tm,),
        in_specs=[pl.BlockSpec((tm, D), lambda i: (i, 0)),
                  pl.BlockSpec((1, D), lambda i: (0, 0))],
        out_specs=pl.BlockSpec((tm, D), lambda i: (i, 0)),
        out_shape=jax.ShapeDtypeStruct((M, D), out_dtype),
        compiler_params=_params(1),
        name="rmsnorm_cast",
    )(x, g.reshape(1, D).astype(F32))


def _ws_matmul_body(*refs, n_w, n_extra, n_out, n_row_tiles, tnw, tk, epilogue):
    it = iter(refs)
    x_ref, xs_ref = next(it), next(it)
    w_refs = [next(it) for _ in range(n_w)]
    extra_refs = [next(it) for _ in range(n_extra)]
    extra_s_refs = [next(it) for _ in range(n_extra)]
    out_refs = [next(it) for _ in range(n_out)]
    out_s_refs = [next(it) for _ in range(n_out)]
    wbf_ref = next(it)
    i = pl.program_id(1)

    @pl.when(i == 0)
    def _():
        rows = min(CAST_ROWS, tk)
        assert tk % rows == 0

        def cast_rows(r, carry):
            sl = pl.ds(pl.multiple_of(r * rows, rows), rows)
            for t, w_ref in enumerate(w_refs):
                wbf_ref[sl, t * tnw:(t + 1) * tnw] = w_ref[sl, :].astype(BF16)
            return carry

        lax.fori_loop(0, tk // rows, cast_rows, 0)

    @pl.when(i < n_row_tiles)
    def _():
        acc = jnp.dot(x_ref[...].astype(BF16), wbf_ref[...], preferred_element_type=F32)
        epilogue(acc, extra_refs, out_refs)

    @pl.when(i == n_row_tiles)
    def _():
        acc = jnp.dot(xs_ref[...].astype(BF16), wbf_ref[...], preferred_element_type=F32)
        epilogue(acc, extra_s_refs, out_s_refs)


def ws_matmul(x, xs, w, col_block_offsets, n_tiles, tm, tnw, epilogue, outs,
              extras=(), k_block=0, tk=None, name="ws_matmul"):
    M, Ms = x.shape[0], xs.shape[0]
    w_all, layer = w
    tk = x.shape[1] if tk is None else tk
    n_w = len(col_block_offsets)
    nrt = M // tm
    row = lambda i: jnp.minimum(i, nrt - 1)
    in_specs = [pl.BlockSpec((tm, tk), lambda j, i: (row(i), k_block)),
                pl.BlockSpec((Ms, tk), lambda j, i: (0, k_block))]
    for off in col_block_offsets:
        in_specs.append(pl.BlockSpec((None, tk, tnw), lambda j, i, off=off: (layer, k_block, off + j)))
    specs_p, specs_s, args_p, args_s = [], [], [], []
    for kind, *arrs in extras:
        if kind == "tile":
            a, a_s = arrs
            wd = a.shape[1] // n_tiles
            specs_p.append(pl.BlockSpec((tm, wd), lambda j, i: (row(i), j)))
            specs_s.append(pl.BlockSpec((Ms, wd), lambda j, i: (0, j)))
        elif kind == "rows":
            a, a_s = arrs
            npb = max(a.shape[0] // tm, 1)
            specs_p.append(pl.BlockSpec((tm, a.shape[1]), lambda j, i, npb=npb: (row(i) % npb, 0)))
            specs_s.append(pl.BlockSpec((Ms, a.shape[1]), lambda j, i: (0, 0)))
        else:
            a, = arrs
            a_s = a
            specs_p.append(pl.BlockSpec(a.shape, lambda j, i: (0, 0)))
            specs_s.append(pl.BlockSpec(a.shape, lambda j, i: (0, 0)))
        args_p.append(a)
        args_s.append(a_s)
    out_specs_p, out_specs_s, shapes_p, shapes_s = [], [], [], []
    for n_cols, dtype in outs:
        wd = n_cols // n_tiles
        out_specs_p.append(pl.BlockSpec((tm, wd), lambda j, i: (row(i), j)))
        out_specs_s.append(pl.BlockSpec((Ms, wd), lambda j, i: (0, j)))
        shapes_p.append(jax.ShapeDtypeStruct((M, n_cols), dtype))
        shapes_s.append(jax.ShapeDtypeStruct((Ms, n_cols), F32))
    body = functools.partial(_ws_matmul_body, n_w=n_w, n_extra=len(extras), n_out=len(outs),
                             n_row_tiles=nrt, tnw=tnw, tk=tk, epilogue=epilogue)
    res = pl.pallas_call(
        body,
        grid=(n_tiles, nrt + 1),
        in_specs=in_specs + specs_p + specs_s,
        out_specs=out_specs_p + out_specs_s,
        out_shape=shapes_p + shapes_s,
        scratch_shapes=[pltpu.VMEM((tk, n_w * tnw), BF16)],
        compiler_params=_params(2),
        name=name,
    )(x, xs, *([w_all] * n_w), *args_p, *args_s)
    n = len(outs)
    return res[:n], res[n:]


def _row_tile(M):
    return min(M, 1024)


def ffn_gate_up(h, hs, w_gu):
    d_ff = w_gu[0].shape[2] // 2
    tnw = 256
    n_tiles = d_ff // tnw

    def epilogue(acc, extra_refs, out_refs):
        gate, up = acc[:, :tnw], acc[:, tnw:]
        out_refs[0][...] = (_silu(gate) * up).astype(out_refs[0].dtype)

    (u,), (us,) = ws_matmul(h, hs, w_gu, (0, n_tiles), n_tiles, _row_tile(h.shape[0]), tnw, epilogue,
                            [(d_ff, BF16)], name="ffn_gate_up")
    return u, us


def matmul_residual(a, a_s, w, res, res_s, alpha, k_splits=1, name="matmul_residual"):
    M, K = a.shape
    N = w[0].shape[2]
    tm = min(M, 512)
    tn = 512
    tk = K // k_splits

    def epilogue(acc, extra_refs, out_refs):
        out_refs[0][...] = extra_refs[0][...] + alpha * acc

    for kb in range(k_splits):
        (res,), (res_s,) = ws_matmul(a, a_s, w, (0,), N // tn, tm, tn, epilogue, [(N, F32)],
                                     extras=[("tile", res, res_s)], k_block=kb, tk=tk, name=name)
    return res, res_s


def swiglu_half_step(x, xs, g, w_gu, w_down):
    h = rmsnorm_cast(x, g, BF16)
    hs = rmsnorm_cast(xs, g, F32)
    u, us = ffn_gate_up(h, hs, w_gu)
    return matmul_residual(u, us, w_down, x, xs, 0.5, k_splits=2, name="ffn_down")


def _rope_tables_full(pos, rot_dim, theta):
    half = rot_dim // 2
    inv = 1.0 / (theta ** (jnp.arange(half, dtype=F32) / half))
    ang = pos.astype(F32)[:, None] * inv[None, :]
    return jnp.cos(ang), jnp.sin(ang)


def _rope_tables_partial(pos, rot_dim, theta, width):
    half = rot_dim // 2
    cos, sin = _rope_tables_full(pos, rot_dim, theta)
    P = pos.shape[0]
    ones = jnp.ones((P, width - rot_dim), F32)
    zeros_rest = jnp.zeros((P, width - rot_dim), F32)
    zeros_half = jnp.zeros((P, half), F32)
    c = jnp.concatenate([cos, cos, ones], axis=1)
    s_lo = jnp.concatenate([-sin, zeros_half, zeros_rest], axis=1)
    s_hi = jnp.concatenate([zeros_half, sin, zeros_rest], axis=1)
    return c, s_lo, s_hi


def ret_project(h, hs, w_qkvg, pos, pos_s):
    M, D = h.shape
    tm = _row_tile(M)
    tn = 2 * RET_DK
    half = RET_DK // 2
    cos, sin = _rope_tables_full(pos, RET_DK, RET_THETA)
    cos_s, sin_s = _rope_tables_full(pos_s, RET_DK, RET_THETA)
    k_scale = RET_DK ** -0.5
    n_head_tiles = D // tn

    def rope_epilogue(acc, extra_refs, out_refs):
        c = extra_refs[0][...]
        s = extra_refs[1][...]
        is_k = pl.program_id(0) >= n_head_tiles
        scale = jnp.where(is_k, k_scale, 1.0).astype(F32)
        for hh in range(tn // RET_DK):
            x1 = acc[:, hh * RET_DK:hh * RET_DK + half]
            x2 = acc[:, hh * RET_DK + half:(hh + 1) * RET_DK]
            out_refs[0][:, hh * RET_DK:hh * RET_DK + half] = (
                (x1 * c - x2 * s) * scale).astype(out_refs[0].dtype)
            out_refs[0][:, hh * RET_DK + half:(hh + 1) * RET_DK] = (
                (x2 * c + x1 * s) * scale).astype(out_refs[0].dtype)

    def plain_epilogue(acc, extra_refs, out_refs):
        out_refs[0][...] = acc.astype(out_refs[0].dtype)

    (qk,), (qk_s,) = ws_matmul(h, hs, w_qkvg, (0,), 2 * n_head_tiles, tm, tn, rope_epilogue,
                               [(2 * D, BF16)], extras=[("rows", cos, cos_s), ("rows", sin, sin_s)],
                               name="ret_qk_proj")
    (v,), (v_s,) = ws_matmul(h, hs, w_qkvg, (2 * n_head_tiles,), n_head_tiles, tm, tn, plain_epilogue,
                             [(D, BF16)], name="ret_v_proj")
    (g,), (g_s,) = ws_matmul(h, hs, w_qkvg, (3 * n_head_tiles,), n_head_tiles, tm, tn, plain_epilogue,
                             [(D, F32)], name="ret_g_proj")
    return (qk, v, g), (qk_s, v_s, g_s)


def _log_gamma_rows():
    lg = jnp.log1p(-(2.0 ** (-5.0 - jnp.arange(RET_HEADS, dtype=F32))))
    return jnp.broadcast_to(lg[:, None, None], (RET_HEADS, 1, RET_DK))


def _group_norm_gate(o, gn, g):
    mu = jnp.mean(o, axis=-1, keepdims=True)
    d = o - mu
    var = jnp.mean(d * d, axis=-1, keepdims=True)
    y = d * lax.rsqrt(var + EPS) * gn
    return _silu(g) * y


RET_HEADS_PER_STEP = 2


def _ret_prompt_body(q_ref, k_ref, v_ref, g_ref, lg_ref, gn_ref, y_ref, s_ref, state_ref, *, n_chunks):
    L, DK = RET_CHUNK, RET_DK
    r = lax.broadcasted_iota(jnp.int32, (L, L), 0)
    c = lax.broadcasted_iota(jnp.int32, (L, L), 1)
    diff = (r - c).astype(F32)
    rows = lax.broadcasted_iota(jnp.int32, (L, DK), 0).astype(F32)
    decays = []
    for hh in range(RET_HEADS_PER_STEP):
        lg = lg_ref[hh]
        decays.append((jnp.where(diff >= 0, jnp.exp(jnp.maximum(diff, 0.0) * lg[:, :L]), 0.0),
                       jnp.exp((rows + 1.0) * lg),
                       jnp.exp((L - 1.0 - rows) * lg),
                       jnp.exp(float(L) * lg)))
    state_ref[...] = jnp.zeros_like(state_ref)

    def chunk(ci, carry):
        sl = pl.ds(pl.multiple_of(ci * L, L), L)
        for hh in range(RET_HEADS_PER_STEP):
            dmat, q_decay, k_decay, chunk_decay = decays[hh]
            hs = slice(hh * DK, (hh + 1) * DK)
            q = q_ref[sl, hs]
            k = k_ref[sl, hs]
            v = v_ref[sl, hs]
            state = state_ref[hh]
            scores = lax.dot_general(q, k, (((1,), (1,)), ((), ())), preferred_element_type=F32) * dmat
            inner = jnp.dot(scores.astype(BF16), v, preferred_element_type=F32)
            cross = jnp.dot(q, state.astype(BF16), preferred_element_type=F32) * q_decay
            o = inner + cross
            kd = (k.astype(F32) * k_decay).astype(BF16)
            state_ref[hh] = chunk_decay * state + lax.dot_general(
                kd, v, (((0,), (0,)), ((), ())), preferred_element_type=F32)
            y_ref[sl, hs] = _group_norm_gate(o, gn_ref[:, hs], g_ref[sl, hs]).astype(y_ref.dtype)
        return carry

    lax.fori_loop(0, n_chunks, chunk, 0)
    s_ref[0] = state_ref[...]


def retention_prompt(qk, v, g, gn_gain, batch, seq):
    M, D = v.shape
    H, DK, HPS = RET_HEADS, RET_DK, RET_HEADS_PER_STEP
    W = HPS * DK
    body = functools.partial(_ret_prompt_body, n_chunks=seq // RET_CHUNK)
    y, s_fin = pl.pallas_call(
        body,
        grid=(batch, H // HPS),
        in_specs=[pl.BlockSpec((seq, W), lambda b, h: (b, h)),
                  pl.BlockSpec((seq, W), lambda b, h: (b, H // HPS + h)),
                  pl.BlockSpec((seq, W), lambda b, h: (b, h)),
                  pl.BlockSpec((seq, W), lambda b, h: (b, h)),
                  pl.BlockSpec((HPS, 1, DK), lambda b, h: (h, 0, 0)),
                  pl.BlockSpec((1, W), lambda b, h: (0, h))],
        out_specs=[pl.BlockSpec((seq, W), lambda b, h: (b, h)),
                   pl.BlockSpec((1, HPS, DK, DK), lambda b, h: (b, h, 0, 0))],
        out_shape=[jax.ShapeDtypeStruct((M, D), BF16),
                   jax.ShapeDtypeStruct((batch, H, DK, DK), F32)],
        scratch_shapes=[pltpu.VMEM((HPS, DK, DK), F32)],
        compiler_params=_params(2),
        name="retention_prompt",
    )(qk, qk, v, g, _log_gamma_rows(), gn_gain.reshape(1, D).astype(F32))
    return y, s_fin


def _ret_step_body(q_ref, k_ref, v_ref, g_ref, lg_ref, gn_ref, s_in_ref, y_ref, s_out_ref, *, batch):
    lg = lg_ref[0]
    d0 = jnp.exp(0.0 * lg)
    q_decay = jnp.exp(1.0 * lg)
    k_decay = jnp.exp(0.0 * lg)
    step_decay = jnp.exp(1.0 * lg)
    rnd = lambda t: t.astype(BF16).astype(F32)
    qf, kf, vf = rnd(q_ref[...]), rnd(k_ref[...]), rnd(v_ref[...])
    row = lax.broadcasted_iota(jnp.int32, qf.shape, 0)
    scores = jnp.sum(qf * kf, axis=-1, keepdims=True) * d0[:, :1]
    inner = rnd(scores) * vf
    kd = rnd(kf * k_decay)
    cross = jnp.zeros(qf.shape, F32)
    for b in range(batch):
        state = s_in_ref[b, 0].astype(F32)
        q_only_b = jnp.where(row == b, qf, 0.0)
        k_only_b = jnp.where(row == b, kd, 0.0)
        cross = cross + jnp.dot(q_only_b, rnd(state), preferred_element_type=F32)
        s_out_ref[b, 0] = step_decay * state + lax.dot_general(
            k_only_b, vf, (((0,), (0,)), ((), ())), preferred_element_type=F32)
    o = inner + cross * q_decay
    y_ref[...] = _group_norm_gate(o, gn_ref[...], g_ref[...]).astype(y_ref.dtype)


def retention_step(qk, v, g, gn_gain, state):
    B, D = v.shape
    H, DK = RET_HEADS, RET_DK
    body = functools.partial(_ret_step_body, batch=B)
    y, s_new = pl.pallas_call(
        body,
        grid=(H,),
        in_specs=[pl.BlockSpec((B, DK), lambda h: (0, h)),
                  pl.BlockSpec((B, DK), lambda h: (0, H + h)),
                  pl.BlockSpec((B, DK), lambda h: (0, h)),
                  pl.BlockSpec((B, DK), lambda h: (0, h)),
                  pl.BlockSpec((1, 1, DK), lambda h: (h, 0, 0)),
                  pl.BlockSpec((1, DK), lambda h: (0, h)),
                  pl.BlockSpec((B, 1, DK, DK), lambda h: (0, h, 0, 0))],
        out_specs=[pl.BlockSpec((B, DK), lambda h: (0, h)),
                   pl.BlockSpec((B, 1, DK, DK), lambda h: (0, h, 0, 0))],
        out_shape=[jax.ShapeDtypeStruct((B, D), F32),
                   jax.ShapeDtypeStruct((B, H, DK, DK), F32)],
        compiler_params=_params(1),
        name="retention_step",
    )(qk, qk, v, g, _log_gamma_rows(), gn_gain.reshape(1, D).astype(F32), state)
    return y, s_new


def _norm_rope_groups(acc, gain, c, s_lo, s_hi, group_ones, n_groups):
    half = ROT_DIM // 2
    sq = acc * acc
    hi = sq.astype(BF16)
    lo = (sq - hi.astype(F32)).astype(BF16)
    ssq = (jnp.dot(hi, group_ones, preferred_element_type=F32)
           + jnp.dot(lo, group_ones, preferred_element_type=F32))
    inv = lax.rsqrt(ssq * (1.0 / DIFF_DH) + EPS)
    outs = []
    for gi in range(n_groups):
        sl = slice(gi * DIFF_DH, (gi + 1) * DIFF_DH)
        yg = acc[:, sl] * inv[:, sl] * gain
        rot = (yg * c + pltpu.roll(yg, DIFF_DH - half, 1) * s_lo + pltpu.roll(yg, half, 1) * s_hi)
        outs.append(rot)
    return outs


def _diff_norm_rope_extras(gain, pos, pos_s, tn):
    tabs = _rope_tables_partial(pos, ROT_DIM, ROPE_THETA, DIFF_DH)
    tabs_s = _rope_tables_partial(pos_s, ROT_DIM, ROPE_THETA, DIFF_DH)
    lane_group = jnp.arange(tn) // DIFF_DH
    group_ones = (lane_group[:, None] == lane_group[None, :]).astype(BF16)
    return ([("const", gain.reshape(1, DIFF_DH).astype(F32))]
            + [("rows", a, a_s) for a, a_s in zip(tabs, tabs_s)]
            + [("const", group_ones)])


def shared_kv(x, xs, g_kv, w_kv, g_k_norm, pos, pos_s):
    h = rmsnorm_cast(x, g_kv, BF16)
    hs = rmsnorm_cast(xs, g_kv, F32)
    tm = _row_tile(x.shape[0])
    tn = 4 * DIFF_DH
    n_groups = tn // DIFF_DH
    nk = (DIFF_HEADS * 2 * DIFF_DH) // tn
    nv = (DIFF_HEADS * DIFF_DV) // tn

    def k_epilogue(acc, extra_refs, out_refs):
        gain, c, s_lo, s_hi, group_ones = (r[...] for r in extra_refs)
        for gi, rot in enumerate(_norm_rope_groups(acc, gain, c, s_lo, s_hi, group_ones, n_groups)):
            out_refs[0][:, gi * DIFF_DH:(gi + 1) * DIFF_DH] = rot
            out_refs[1][:, gi * DIFF_DH:(gi + 1) * DIFF_DH] = rot.astype(out_refs[1].dtype)

    def v_epilogue(acc, extra_refs, out_refs):
        out_refs[0][...] = acc
        out_refs[1][...] = acc.astype(out_refs[1].dtype)

    (k32, kbf), (k32_s, _) = ws_matmul(h, hs, w_kv, (0,), nk, tm, tn, k_epilogue,
                                       [(nk * tn, F32), (nk * tn, BF16)],
                                       extras=_diff_norm_rope_extras(g_k_norm, pos, pos_s, tn),
                                       name="shared_k_proj")
    (v32, vbf), (v32_s, _) = ws_matmul(h, hs, w_kv, (nk,), nv, tm, tn, v_epilogue,
                                       [(nv * tn, F32), (nv * tn, BF16)], name="shared_v_proj")
    return (k32, kbf, v32, vbf), (k32_s, v32_s)


def diff_query(h, hs, w_q, g_q_norm, pos, pos_s):
    tm = _row_tile(h.shape[0])
    tn = 4 * DIFF_DH
    n_groups = tn // DIFF_DH
    q_scale = DIFF_DH ** -0.5

    def q_epilogue(acc, extra_refs, out_refs):
        gain, c, s_lo, s_hi, group_ones = (r[...] for r in extra_refs)
        for gi, rot in enumerate(_norm_rope_groups(acc, gain, c, s_lo, s_hi, group_ones, n_groups)):
            out_refs[0][:, gi * DIFF_DH:(gi + 1) * DIFF_DH] = (rot * q_scale).astype(out_refs[0].dtype)

    N = w_q[0].shape[2]
    (q,), (q_s,) = ws_matmul(h, hs, w_q, (0,), N // tn, tm, tn, q_epilogue, [(N, BF16)],
                             extras=_diff_norm_rope_extras(g_q_norm, pos, pos_s, tn), name="diff_q_proj")
    return q, q_s


def _lambda_value(lam_ref, lam_init):
    l = lam_ref[...]
    a = jnp.sum(l[0:1, :] * l[1:2, :], axis=-1, keepdims=True)
    b = jnp.sum(l[2:3, :] * l[3:4, :], axis=-1, keepdims=True)
    return jnp.exp(a) - jnp.exp(b) + lam_init


def _sub_norm(o, g_sub, lam_init):
    y = o * lax.rsqrt(jnp.mean(o * o, axis=-1, keepdims=True) + EPS) * g_sub
    return y * (1.0 - lam_init)


def _diff_attn_body(q_ref, k_ref, v_ref, lam_ref, gsub_ref, o_ref, m_ref, l_ref, acc_ref, *, tq, lam_init):
    qi = pl.program_id(2)
    tk = tq
    DH, DV = DIFF_DH, DIFF_DV
    m_ref[...] = jnp.full(m_ref.shape, -jnp.inf, F32)
    l_ref[...] = jnp.zeros(l_ref.shape, F32)
    acc_ref[...] = jnp.zeros(acc_ref.shape, F32)

    def update(j, masked):
        ksl = pl.ds(pl.multiple_of(j * tk, tk), tk)
        kblk = k_ref[ksl, :]
        vblk = v_ref[ksl, :]
        for c in range(2):
            s = lax.dot_general(q_ref[:, c * DH:(c + 1) * DH], kblk[:, c * DH:(c + 1) * DH],
                                (((1,), (1,)), ((), ())), preferred_element_type=F32)
            if masked:
                rr = lax.broadcasted_iota(jnp.int32, (tq, tk), 0)
                cc = lax.broadcasted_iota(jnp.int32, (tq, tk), 1)
                s = jnp.where(cc <= rr, s, -jnp.inf)
            m_prev = m_ref[c]
            m_new = jnp.maximum(m_prev, jnp.max(s, axis=-1, keepdims=True))
            corr = jnp.exp(m_prev - m_new)
            p = jnp.exp(s - jnp.concatenate([m_new] * (tk // LANES), axis=1))
            p_lanes = p[:, 0:LANES]
            for t in range(1, tk // LANES):
                p_lanes = p_lanes + p[:, t * LANES:(t + 1) * LANES]
            l_ref[c] = l_ref[c] * corr + p_lanes
            acc_ref[c] = (acc_ref[c] * jnp.concatenate([corr] * (DV // LANES), axis=1)
                          + jnp.dot(p.astype(BF16), vblk, preferred_element_type=F32))
            m_ref[c] = m_new

    def body(j, carry):
        update(j, False)
        return carry

    lax.fori_loop(0, qi, body, 0)
    update(qi, True)

    lam = _lambda_value(lam_ref, lam_init)
    l0 = jnp.sum(l_ref[0], axis=-1, keepdims=True)
    l1 = jnp.sum(l_ref[1], axis=-1, keepdims=True)
    o = acc_ref[0] / l0 - lam * (acc_ref[1] / l1)
    o_ref[...] = _sub_norm(o, gsub_ref[...], lam_init).astype(o_ref.dtype)


def diff_attn_prompt(q, k, v, lam_rows, g_sub, lam_init, batch, seq, tq=512):
    M, D = q.shape
    H, DV = DIFF_HEADS, DIFF_DV
    tq = min(seq, tq)
    nq = seq // tq
    body = functools.partial(_diff_attn_body, tq=tq, lam_init=lam_init)
    return pl.pallas_call(
        body,
        grid=(batch, H, nq),
        in_specs=[pl.BlockSpec((tq, DV), lambda b, h, i: (b * nq + i, h)),
                  pl.BlockSpec((seq, DV), lambda b, h, i: (b, h)),
                  pl.BlockSpec((seq, DV), lambda b, h, i: (b, h)),
                  pl.BlockSpec((4, DIFF_DH), lambda b, h, i: (0, 0)),
                  pl.BlockSpec((1, DV), lambda b, h, i: (0, 0))],
        out_specs=pl.BlockSpec((tq, DV), lambda b, h, i: (b * nq + i, h)),
        out_shape=jax.ShapeDtypeStruct((M, D), BF16),
        scratch_shapes=[pltpu.VMEM((2, tq, LANES), F32),
                        pltpu.VMEM((2, tq, LANES), F32),
                        pltpu.VMEM((2, tq, DV), F32)],
        compiler_params=_params(3),
        name="diff_attn_prompt",
    )(q, k, v, lam_rows, g_sub.reshape(1, DV).astype(F32))


def _diff_decode_body(pt_ref, qt_ref, *refs, pages_per_step, lam_init):
    k_refs = refs[:pages_per_step]
    v_refs = refs[pages_per_step:2 * pages_per_step]
    knew_ref, vnew_ref, lam_ref, gsub_ref, o_ref, m_ref, l_ref, acc_ref = refs[2 * pages_per_step:]
    step = pl.program_id(1)
    n_steps = pl.num_programs(1)
    H = DIFF_HEADS
    NQ = 2 * H

    @pl.when(step == 0)
    def _():
        m_ref[...] = jnp.full(m_ref.shape, -jnp.inf, F32)
        l_ref[...] = jnp.zeros(l_ref.shape, F32)
        acc_ref[...] = jnp.zeros(acc_ref.shape, F32)

    qt = qt_ref[0]

    def update(kb, vb, n_real_rows):
        n_rows = kb.shape[0]
        st = lax.dot_general(qt, kb, (((1,), (1,)), ((), ())), preferred_element_type=F32)
        rr = lax.broadcasted_iota(jnp.int32, (NQ, n_rows), 0)
        cc = lax.broadcasted_iota(jnp.int32, (NQ, n_rows), 1)
        valid = (cc % H) == (rr % H)
        if n_real_rows < n_rows:
            valid = valid & (cc < n_real_rows)
        st = jnp.where(valid, st, -jnp.inf)
        m_prev = m_ref[...]
        m_new = jnp.maximum(m_prev, jnp.max(st, axis=-1, keepdims=True))
        corr = jnp.exp(m_prev - m_new)
        p = jnp.exp(st - m_new[:, :1])
        l_ref[...] = l_ref[...] * corr + jnp.sum(p, axis=-1, keepdims=True)
        acc_ref[...] = acc_ref[...] * corr[:, :1] + jnp.dot(p.astype(BF16), vb, preferred_element_type=F32)
        m_ref[...] = m_new

    for r in range(pages_per_step):
        update(k_refs[r][0].astype(BF16), v_refs[r][0].astype(BF16), PAGE_SIZE * H)

    @pl.when(step == n_steps - 1)
    def _():
        pad = jnp.zeros((LANES - H, knew_ref.shape[2]), F32)
        kn = jnp.concatenate([knew_ref[0], pad], axis=0).astype(BF16)
        vn = jnp.concatenate([vnew_ref[0], pad], axis=0).astype(BF16)
        update(kn, vn, H)
        lam = _lambda_value(lam_ref, lam_init)
        acc = acc_ref[...]
        l = l_ref[...]
        o = acc[0:H] / l[0:H, :1] - lam * (acc[H:NQ] / l[H:NQ, :1])
        o_ref[0] = _sub_norm(o, gsub_ref[...], lam_init)


def diff_attn_decode(q, k_new, v_new, cache_k, cache_v, page_table, lam_rows, g_sub, lam_init):
    B, D = q.shape
    H, DH, DV = DIFF_HEADS, DIFF_DH, DIFF_DV
    n_pool = cache_k.shape[0]
    n_pages = page_table.shape[1]
    pages_per_step = 2
    n_steps = n_pages // pages_per_step
    rows = PAGE_SIZE * H
    ck = cache_k.reshape(n_pool, rows, 2 * DH)
    cv = cache_v.reshape(n_pool, rows, DV)
    q4 = q.reshape(B, H, 2, DH)
    zero = jnp.zeros((B, H, DH), q.dtype)
    qt = jnp.concatenate([jnp.concatenate([q4[:, :, 0], zero], axis=-1),
                          jnp.concatenate([zero, q4[:, :, 1]], axis=-1)], axis=1).astype(BF16)

    def page_spec(r, width):
        return pl.BlockSpec((1, rows, width), lambda b, s, pt: (pt[b, s * pages_per_step + r], 0, 0))

    body = functools.partial(_diff_decode_body, pages_per_step=pages_per_step, lam_init=lam_init)
    grid_spec = pltpu.PrefetchScalarGridSpec(
        num_scalar_prefetch=1,
        grid=(B, n_steps),
        in_specs=[pl.BlockSpec((1, 2 * H, 2 * DH), lambda b, s, pt: (b, 0, 0))]
                 + [page_spec(r, 2 * DH) for r in range(pages_per_step)]
                 + [page_spec(r, DV) for r in range(pages_per_step)]
                 + [pl.BlockSpec((1, H, 2 * DH), lambda b, s, pt: (b, 0, 0)),
                    pl.BlockSpec((1, H, DV), lambda b, s, pt: (b, 0, 0)),
                    pl.BlockSpec((4, DH), lambda b, s, pt: (0, 0)),
                    pl.BlockSpec((1, DV), lambda b, s, pt: (0, 0))],
        out_specs=pl.BlockSpec((1, H, DV), lambda b, s, pt: (b, 0, 0)),
        scratch_shapes=[pltpu.VMEM((2 * H, LANES), F32),
                        pltpu.VMEM((2 * H, LANES), F32),
                        pltpu.VMEM((2 * H, DV), F32)],
    )
    out = pl.pallas_call(
        body,
        grid_spec=grid_spec,
        out_shape=jax.ShapeDtypeStruct((B, H, DV), F32),
        compiler_params=_params(2),
        name="diff_attn_decode",
    )(page_table, qt, *([ck] * pages_per_step), *([cv] * pages_per_step),
      k_new.reshape(B, H, 2 * DH), v_new.reshape(B, H, DV), lam_rows, g_sub.reshape(1, DV).astype(F32))
    return out.reshape(B, H * DV)


def kernel(x_prompt, x_sample, state_ret, cache_k, cache_v, page_table, g_ffn1, w_ffn1_gu, w_ffn1_down, g_mix, g_ffn2, w_ffn2_gu, w_ffn2_down, w_ret_qkvg, g_ret_gn, w_ret_o, g_kv, w_kv, g_k_norm, w_diff_q, g_q_norm, lambda_q1, lambda_k1, lambda_q2, lambda_k2, g_sub, w_diff_o):
    B, S, D = x_prompt.shape
    Bd, Sd, _ = x_sample.shape
    pos_p = jnp.arange(S)
    pos_s = jnp.broadcast_to(PAST_LEN + jnp.arange(Sd), (Bd * Sd,))
    xp = x_prompt.reshape(B * S, D)
    xs = x_sample.reshape(Bd * Sd, D)
    ret_p, ret_s = [], []
    kp32 = vp32 = ks32 = vs32 = kp = vp = None
    for i in range(DEPTH):
        if i == N_A_LAYERS:
            (kp32, kp, vp32, vp), (ks32, vs32) = shared_kv(xp, xs, g_kv, (w_kv[None], 0), g_k_norm, pos_p, pos_s)
        xp, xs = swiglu_half_step(xp, xs, g_ffn1[i], (w_ffn1_gu, i), (w_ffn1_down, i))
        if i < N_A_LAYERS:
            a = i
            (qk, v, g), (qk_s, v_s, g_s) = ret_project(
                rmsnorm_cast(xp, g_mix[i], BF16), rmsnorm_cast(xs, g_mix[i], F32), (w_ret_qkvg, a), pos_p, pos_s)
            y, s_fin = retention_prompt(qk, v, g, g_ret_gn[a], B, S)
            y_s, s_new = retention_step(qk_s, v_s, g_s, g_ret_gn[a], state_ret[a])
            xp, xs = matmul_residual(y, y_s, (w_ret_o, a), xp, xs, 1.0, name="ret_out_proj")
            ret_p.append(s_fin)
            ret_s.append(s_new)
        else:
            bidx = i - N_A_LAYERS
            lam_init = 0.8 - 0.6 * math.exp(-0.3 * i)
            lam_rows = jnp.stack([lambda_q1[bidx], lambda_k1[bidx],
                                  lambda_q2[bidx], lambda_k2[bidx]]).astype(F32)
            q, q_s = diff_query(rmsnorm_cast(xp, g_mix[i], BF16), rmsnorm_cast(xs, g_mix[i], F32),
                                (w_diff_q, bidx), g_q_norm[bidx], pos_p, pos_s)
            o = diff_attn_prompt(q, kp, vp, lam_rows, g_sub[bidx], lam_init, B, S)
            o_s = diff_attn_decode(q_s, ks32, vs32, cache_k, cache_v, page_table, lam_rows, g_sub[bidx], lam_init)
            xp, xs = matmul_residual(o, o_s, (w_diff_o, bidx), xp, xs, 1.0, name="diff_out_proj")
        xp, xs = swiglu_half_step(xp, xs, g_ffn2[i], (w_ffn2_gu, i), (w_ffn2_down, i))
    return (xp.reshape(B, S, D), xs.reshape(Bd, Sd, D),
            jnp.stack(ret_p), jnp.stack(ret_s),
            kp32.reshape(B, S, DIFF_HEADS, 2 * DIFF_DH), vp32.reshape(B, S, DIFF_HEADS, DIFF_DV),
            ks32.reshape(Bd, Sd, DIFF_HEADS, 2 * DIFF_DH), vs32.reshape(Bd, Sd, DIFF_HEADS, DIFF_DV))
```

```python
import functools
import math

import jax
import jax.numpy as jnp
from jax import lax
from jax.experimental import pallas as pl
from jax.experimental.pallas import tpu as pltpu

F32 = jnp.float32
BF16 = jnp.bfloat16

D_MODEL = 4096
SEQ = 2048
DEPTH = 2
PAST_LEN = 16384
PAGE_SIZE = 128
N_A_LAYERS = DEPTH // 2
RET_HEADS = 16
RET_DK = D_MODEL // RET_HEADS
RET_CHUNK = 128
RET_THETA = 10000.0
DIFF_HEADS = 16
DIFF_DH = D_MODEL // (2 * DIFF_HEADS)
DIFF_DV = 2 * DIFF_DH
ROT_DIM = DIFF_DH // 4
ROPE_THETA = 500000.0
D_FF = 256 * ((8 * D_MODEL // 3 + 255) // 256)
EPS = 1e-6

LANES = 128
VMEM_LIMIT_BYTES = 56 * 1024 * 1024


def _params(n_grid_dims):
    return pltpu.CompilerParams(
        dimension_semantics=("arbitrary",) * n_grid_dims,
        vmem_limit_bytes=VMEM_LIMIT_BYTES)


def _silu(x):
    return x * jax.nn.sigmoid(x)


def _rmsnorm_body(x_ref, g_ref, o_ref):
    x = x_ref[...].astype(F32)
    y = x * lax.rsqrt(jnp.mean(x * x, axis=-1, keepdims=True) + EPS)
    o_ref[...] = (y * g_ref[...]).astype(o_ref.dtype)


def rmsnorm_cast(x, g, out_dtype):
    M, D = x.shape
    tm = min(M, 512)
    return pl.pallas_call(
        _rmsnorm_body,
        grid=(M // tm,),
        in_specs=[pl.BlockSpec((tm, D), lambda i: (i, 0)),
                  pl.BlockSpec((1, D), lambda i: (0, 0))],
        out_specs=pl.BlockSpec((tm, D), lambda i: (i, 0)),
        out_shape=jax.ShapeDtypeStruct((M, D), out_dtype),
        compiler_params=_params(1),
        name="rmsnorm_cast",
    )(x, g.reshape(1, D).astype(F32))


def _ws_matmul_body(*refs, n_w, n_extra, n_out, n_row_tiles, epilogue):
    it = iter(refs)
    x_ref, xs_ref = next(it), next(it)
    w_refs = [next(it) for _ in range(n_w)]
    extra_refs = [next(it) for _ in range(n_extra)]
    extra_s_refs = [next(it) for _ in range(n_extra)]
    out_refs = [next(it) for _ in range(n_out)]
    out_s_refs = [next(it) for _ in range(n_out)]

    def weights_bf16():
        return jnp.concatenate([w_ref[...].astype(BF16) for w_ref in w_refs], axis=1)

    acc = jnp.dot(x_ref[...].astype(BF16), weights_bf16(), preferred_element_type=F32)
    epilogue(acc, extra_refs, out_refs)

    @pl.when(pl.program_id(1) == n_row_tiles - 1)
    def _():
        acc_s = jnp.dot(xs_ref[...].astype(BF16), weights_bf16(), preferred_element_type=F32)
        epilogue(acc_s, extra_s_refs, out_s_refs)


def ws_matmul(x, xs, w, col_block_offsets, n_tiles, tm, tnw, epilogue, outs,
              extras=(), k_block=0, tk=None, name="ws_matmul"):
    M, Ms = x.shape[0], xs.shape[0]
    w_all, layer = w
    tk = x.shape[1] if tk is None else tk
    n_w = len(col_block_offsets)
    nrt = M // tm
    row = lambda j, i: jnp.where(j % 2 == 0, i, nrt - 1 - i)
    in_specs = [pl.BlockSpec((tm, tk), lambda j, i: (row(j, i), k_block)),
                pl.BlockSpec((Ms, tk), lambda j, i: (0, k_block))]
    for off in col_block_offsets:
        in_specs.append(pl.BlockSpec((None, tk, tnw), lambda j, i, off=off: (layer, k_block, off + j)))
    specs_p, specs_s, args_p, args_s = [], [], [], []
    for kind, *arrs in extras:
        if kind == "tile":
            a, a_s = arrs
            wd = a.shape[1] // n_tiles
            specs_p.append(pl.BlockSpec((tm, wd), lambda j, i: (row(j, i), j)))
            specs_s.append(pl.BlockSpec((Ms, wd), lambda j, i: (0, j)))
        elif kind == "rows":
            a, a_s = arrs
            npb = max(a.shape[0] // tm, 1)
            specs_p.append(pl.BlockSpec((tm, a.shape[1]), lambda j, i, npb=npb: (row(j, i) % npb, 0)))
            specs_s.append(pl.BlockSpec((Ms, a.shape[1]), lambda j, i: (0, 0)))
        else:
            a, = arrs
            a_s = a
            specs_p.append(pl.BlockSpec(a.shape, lambda j, i: (0, 0)))
            specs_s.append(pl.BlockSpec(a.shape, lambda j, i: (0, 0)))
        args_p.append(a)
        args_s.append(a_s)
    out_specs_p, out_specs_s, shapes_p, shapes_s = [], [], [], []
    for n_cols, dtype in outs:
        wd = n_cols // n_tiles
        out_specs_p.append(pl.BlockSpec((tm, wd), lambda j, i: (row(j, i), j)))
        out_specs_s.append(pl.BlockSpec((Ms, wd), lambda j, i: (0, j)))
        shapes_p.append(jax.ShapeDtypeStruct((M, n_cols), dtype))
        shapes_s.append(jax.ShapeDtypeStruct((Ms, n_cols), F32))
    body = functools.partial(_ws_matmul_body, n_w=n_w, n_extra=len(extras), n_out=len(outs),
                             n_row_tiles=nrt, epilogue=epilogue)
    res = pl.pallas_call(
        body,
        grid=(n_tiles, nrt),
        in_specs=in_specs + specs_p + specs_s,
        out_specs=out_specs_p + out_specs_s,
        out_shape=shapes_p + shapes_s,
        compiler_params=_params(2),
        name=name,
    )(x, xs, *([w_all] * n_w), *args_p, *args_s)
    n = len(outs)
    return res[:n], res[n:]


def _row_tile(M):
    return min(M, 1024)


def ffn_gate_up(h, hs, w_gu):
    d_ff = w_gu[0].shape[2] // 2
    tnw = 256
    n_tiles = d_ff // tnw

    def epilogue(acc, extra_refs, out_refs):
        gate, up = acc[:, :tnw], acc[:, tnw:]
        out_refs[0][...] = (_silu(gate) * up).astype(out_refs[0].dtype)

    (u,), (us,) = ws_matmul(h, hs, w_gu, (0, n_tiles), n_tiles, _row_tile(h.shape[0]), tnw, epilogue,
                            [(d_ff, BF16)], name="ffn_gate_up")
    return u, us


def matmul_residual(a, a_s, w, res, res_s, alpha, k_splits=1, name="matmul_residual"):
    M, K = a.shape
    N = w[0].shape[2]
    tm = min(M, 512)
    tn = 512
    tk = K // k_splits

    def epilogue(acc, extra_refs, out_refs):
        out_refs[0][...] = extra_refs[0][...] + alpha * acc

    for kb in range(k_splits):
        (res,), (res_s,) = ws_matmul(a, a_s, w, (0,), N // tn, tm, tn, epilogue, [(N, F32)],
                                     extras=[("tile", res, res_s)], k_block=kb, tk=tk, name=name)
    return res, res_s


def swiglu_half_step(x, xs, g, w_gu, w_down):
    h = rmsnorm_cast(x, g, BF16)
    hs = rmsnorm_cast(xs, g, F32)
    u, us = ffn_gate_up(h, hs, w_gu)
    return matmul_residual(u, us, w_down, x, xs, 0.5, k_splits=2, name="ffn_down")


def _rope_tables_full(pos, rot_dim, theta):
    half = rot_dim // 2
    inv = 1.0 / (theta ** (jnp.arange(half, dtype=F32) / half))
    ang = pos.astype(F32)[:, None] * inv[None, :]
    return jnp.cos(ang), jnp.sin(ang)


def _rope_tables_partial(pos, rot_dim, theta, width):
    half = rot_dim // 2
    cos, sin = _rope_tables_full(pos, rot_dim, theta)
    P = pos.shape[0]
    ones = jnp.ones((P, width - rot_dim), F32)
    zeros_rest = jnp.zeros((P, width - rot_dim), F32)
    zeros_half = jnp.zeros((P, half), F32)
    c = jnp.concatenate([cos, cos, ones], axis=1)
    s_lo = jnp.concatenate([-sin, zeros_half, zeros_rest], axis=1)
    s_hi = jnp.concatenate([zeros_half, sin, zeros_rest], axis=1)
    return c, s_lo, s_hi


def ret_project(h, hs, w_qkvg, pos, pos_s):
    M, D = h.shape
    tm = _row_tile(M)
    tn = 2 * RET_DK
    half = RET_DK // 2
    cos, sin = _rope_tables_full(pos, RET_DK, RET_THETA)
    cos_s, sin_s = _rope_tables_full(pos_s, RET_DK, RET_THETA)
    k_scale = RET_DK ** -0.5
    n_head_tiles = D // tn

    def rope_epilogue(acc, extra_refs, out_refs):
        c = extra_refs[0][...]
        s = extra_refs[1][...]
        is_k = pl.program_id(0) >= n_head_tiles
        scale = jnp.where(is_k, k_scale, 1.0).astype(F32)
        for hh in range(tn // RET_DK):
            x1 = acc[:, hh * RET_DK:hh * RET_DK + half]
            x2 = acc[:, hh * RET_DK + half:(hh + 1) * RET_DK]
            out_refs[0][:, hh * RET_DK:hh * RET_DK + half] = (
                (x1 * c - x2 * s) * scale).astype(out_refs[0].dtype)
            out_refs[0][:, hh * RET_DK + half:(hh + 1) * RET_DK] = (
                (x2 * c + x1 * s) * scale).astype(out_refs[0].dtype)

    def plain_epilogue(acc, extra_refs, out_refs):
        out_refs[0][...] = acc.astype(out_refs[0].dtype)

    (qk,), (qk_s,) = ws_matmul(h, hs, w_qkvg, (0,), 2 * n_head_tiles, tm, tn, rope_epilogue,
                               [(2 * D, BF16)], extras=[("rows", cos, cos_s), ("rows", sin, sin_s)],
                               name="ret_qk_proj")
    (v,), (v_s,) = ws_matmul(h, hs, w_qkvg, (2 * n_head_tiles,), n_head_tiles, tm, tn, plain_epilogue,
                             [(D, BF16)], name="ret_v_proj")
    (g,), (g_s,) = ws_matmul(h, hs, w_qkvg, (3 * n_head_tiles,), n_head_tiles, tm, tn, plain_epilogue,
                             [(D, F32)], name="ret_g_proj")
    return (qk, v, g), (qk_s, v_s, g_s)


def _log_gamma_rows():
    lg = jnp.log1p(-(2.0 ** (-5.0 - jnp.arange(RET_HEADS, dtype=F32))))
    return jnp.broadcast_to(lg[:, None, None], (RET_HEADS, 1, RET_DK))


def _group_norm_gate(o, gn, g):
    mu = jnp.mean(o, axis=-1, keepdims=True)
    d = o - mu
    var = jnp.mean(d * d, axis=-1, keepdims=True)
    y = d * lax.rsqrt(var + EPS) * gn
    return _silu(g) * y


RET_HEADS_PER_STEP = 2


def _ret_prompt_body(q_ref, k_ref, v_ref, g_ref, lg_ref, gn_ref, y_ref, s_ref, state_ref, *, n_chunks):
    L, DK = RET_CHUNK, RET_DK
    r = lax.broadcasted_iota(jnp.int32, (L, L), 0)
    c = lax.broadcasted_iota(jnp.int32, (L, L), 1)
    diff = (r - c).astype(F32)
    rows = lax.broadcasted_iota(jnp.int32, (L, DK), 0).astype(F32)
    decays = []
    for hh in range(RET_HEADS_PER_STEP):
        lg = lg_ref[hh]
        decays.append((jnp.where(diff >= 0, jnp.exp(jnp.maximum(diff, 0.0) * lg[:, :L]), 0.0),
                       jnp.exp((rows + 1.0) * lg),
                       jnp.exp((L - 1.0 - rows) * lg),
                       jnp.exp(float(L) * lg)))
    state_ref[...] = jnp.zeros_like(state_ref)

    def chunk(ci, carry):
        sl = pl.ds(pl.multiple_of(ci * L, L), L)
        for hh in range(RET_HEADS_PER_STEP):
            dmat, q_decay, k_decay, chunk_decay = decays[hh]
            hs = slice(hh * DK, (hh + 1) * DK)
            q = q_ref[sl, hs]
            k = k_ref[sl, hs]
            v = v_ref[sl, hs]
            state = state_ref[hh]
            scores = lax.dot_general(q, k, (((1,), (1,)), ((), ())), preferred_element_type=F32) * dmat
            inner = jnp.dot(scores.astype(BF16), v, preferred_element_type=F32)
            cross = jnp.dot(q, state.astype(BF16), preferred_element_type=F32) * q_decay
            o = inner + cross
            kd = (k.astype(F32) * k_decay).astype(BF16)
            state_ref[hh] = chunk_decay * state + lax.dot_general(
                kd, v, (((0,), (0,)), ((), ())), preferred_element_type=F32)
            y_ref[sl, hs] = _group_norm_gate(o, gn_ref[:, hs], g_ref[sl, hs]).astype(y_ref.dtype)
        return carry

    lax.fori_loop(0, n_chunks, chunk, 0)
    s_ref[0] = state_ref[...]


def retention_prompt(qk, v, g, gn_gain, batch, seq):
    M, D = v.shape
    H, DK, HPS = RET_HEADS, RET_DK, RET_HEADS_PER_STEP
    W = HPS * DK
    body = functools.partial(_ret_prompt_body, n_chunks=seq // RET_CHUNK)
    y, s_fin = pl.pallas_call(
        body,
        grid=(batch, H // HPS),
        in_specs=[pl.BlockSpec((seq, W), lambda b, h: (b, h)),
                  pl.BlockSpec((seq, W), lambda b, h: (b, H // HPS + h)),
                  pl.BlockSpec((seq, W), lambda b, h: (b, h)),
                  pl.BlockSpec((seq, W), lambda b, h: (b, h)),
                  pl.BlockSpec((HPS, 1, DK), lambda b, h: (h, 0, 0)),
                  pl.BlockSpec((1, W), lambda b, h: (0, h))],
        out_specs=[pl.BlockSpec((seq, W), lambda b, h: (b, h)),
                   pl.BlockSpec((1, HPS, DK, DK), lambda b, h: (b, h, 0, 0))],
        out_shape=[jax.ShapeDtypeStruct((M, D), BF16),
                   jax.ShapeDtypeStruct((batch, H, DK, DK), F32)],
        scratch_shapes=[pltpu.VMEM((HPS, DK, DK), F32)],
        compiler_params=_params(2),
        name="retention_prompt",
    )(qk, qk, v, g, _log_gamma_rows(), gn_gain.reshape(1, D).astype(F32))
    return y, s_fin


def _ret_step_body(q_ref, k_ref, v_ref, g_ref, lg_ref, gn_ref, s_in_ref, y_ref, s_out_ref, *, batch):
    lg = lg_ref[0]
    d0 = jnp.exp(0.0 * lg)
    q_decay = jnp.exp(1.0 * lg)
    k_decay = jnp.exp(0.0 * lg)
    step_decay = jnp.exp(1.0 * lg)
    rnd = lambda t: t.astype(BF16).astype(F32)
    qf, kf, vf = rnd(q_ref[...]), rnd(k_ref[...]), rnd(v_ref[...])
    row = lax.broadcasted_iota(jnp.int32, qf.shape, 0)
    scores = jnp.sum(qf * kf, axis=-1, keepdims=True) * d0[:, :1]
    inner = rnd(scores) * vf
    kd = rnd(kf * k_decay)
    cross = jnp.zeros(qf.shape, F32)
    for b in range(batch):
        state = s_in_ref[b, 0].astype(F32)
        q_only_b = jnp.where(row == b, qf, 0.0)
        k_only_b = jnp.where(row == b, kd, 0.0)
        cross = cross + jnp.dot(q_only_b, rnd(state), preferred_element_type=F32)
        s_out_ref[b, 0] = step_decay * state + lax.dot_general(
            k_only_b, vf, (((0,), (0,)), ((), ())), preferred_element_type=F32)
    o = inner + cross * q_decay
    y_ref[...] = _group_norm_gate(o, gn_ref[...], g_ref[...]).astype(y_ref.dtype)


def retention_step(qk, v, g, gn_gain, state):
    B, D = v.shape
    H, DK = RET_HEADS, RET_DK
    body = functools.partial(_ret_step_body, batch=B)
    y, s_new = pl.pallas_call(
        body,
        grid=(H,),
        in_specs=[pl.BlockSpec((B, DK), lambda h: (0, h)),
                  pl.BlockSpec((B, DK), lambda h: (0, H + h)),
                  pl.BlockSpec((B, DK), lambda h: (0, h)),
                  pl.BlockSpec((B, DK), lambda h: (0, h)),
                  pl.BlockSpec((1, 1, DK), lambda h: (h, 0, 0)),
                  pl.BlockSpec((1, DK), lambda h: (0, h)),
                  pl.BlockSpec((B, 1, DK, DK), lambda h: (0, h, 0, 0))],
        out_specs=[pl.BlockSpec((B, DK), lambda h: (0, h)),
                   pl.BlockSpec((B, 1, DK, DK), lambda h: (0, h, 0, 0))],
        out_shape=[jax.ShapeDtypeStruct((B, D), F32),
                   jax.ShapeDtypeStruct((B, H, DK, DK), F32)],
        compiler_params=_params(1),
        name="retention_step",
    )(qk, qk, v, g, _log_gamma_rows(), gn_gain.reshape(1, D).astype(F32), state)
    return y, s_new


def _norm_rope_groups(acc, gain, c, s_lo, s_hi, group_ones, n_groups):
    half = ROT_DIM // 2
    sq = acc * acc
    hi = sq.astype(BF16)
    lo = (sq - hi.astype(F32)).astype(BF16)
    ssq = (jnp.dot(hi, group_ones, preferred_element_type=F32)
           + jnp.dot(lo, group_ones, preferred_element_type=F32))
    inv = lax.rsqrt(ssq * (1.0 / DIFF_DH) + EPS)
    outs = []
    for gi in range(n_groups):
        sl = slice(gi * DIFF_DH, (gi + 1) * DIFF_DH)
        yg = acc[:, sl] * inv[:, sl] * gain
        rot = (yg * c + pltpu.roll(yg, DIFF_DH - half, 1) * s_lo + pltpu.roll(yg, half, 1) * s_hi)
        outs.append(rot)
    return outs


def _diff_norm_rope_extras(gain, pos, pos_s, tn):
    tabs = _rope_tables_partial(pos, ROT_DIM, ROPE_THETA, DIFF_DH)
    tabs_s = _rope_tables_partial(pos_s, ROT_DIM, ROPE_THETA, DIFF_DH)
    lane_group = jnp.arange(tn) // DIFF_DH
    group_ones = (lane_group[:, None] == lane_group[None, :]).astype(BF16)
    return ([("const", gain.reshape(1, DIFF_DH).astype(F32))]
            + [("rows", a, a_s) for a, a_s in zip(tabs, tabs_s)]
            + [("const", group_ones)])


def shared_kv(x, xs, g_kv, w_kv, g_k_norm, pos, pos_s):
    h = rmsnorm_cast(x, g_kv, BF16)
    hs = rmsnorm_cast(xs, g_kv, F32)
    tm = _row_tile(x.shape[0])
    tn = 4 * DIFF_DH
    n_groups = tn // DIFF_DH
    nk = (DIFF_HEADS * 2 * DIFF_DH) // tn
    nv = (DIFF_HEADS * DIFF_DV) // tn

    def k_epilogue(acc, extra_refs, out_refs):
        gain, c, s_lo, s_hi, group_ones = (r[...] for r in extra_refs)
        for gi, rot in enumerate(_norm_rope_groups(acc, gain, c, s_lo, s_hi, group_ones, n_groups)):
            out_refs[0][:, gi * DIFF_DH:(gi + 1) * DIFF_DH] = rot
            out_refs[1][:, gi * DIFF_DH:(gi + 1) * DIFF_DH] = rot.astype(out_refs[1].dtype)

    def v_epilogue(acc, extra_refs, out_refs):
        out_refs[0][...] = acc
        out_refs[1][...] = acc.astype(out_refs[1].dtype)

    (k32, kbf), (k32_s, _) = ws_matmul(h, hs, w_kv, (0,), nk, tm, tn, k_epilogue,
                                       [(nk * tn, F32), (nk * tn, BF16)],
                                       extras=_diff_norm_rope_extras(g_k_norm, pos, pos_s, tn),
                                       name="shared_k_proj")
    (v32, vbf), (v32_s, _) = ws_matmul(h, hs, w_kv, (nk,), nv, tm, tn, v_epilogue,
                                       [(nv * tn, F32), (nv * tn, BF16)], name="shared_v_proj")
    return (k32, kbf, v32, vbf), (k32_s, v32_s)


def diff_query(h, hs, w_q, g_q_norm, pos, pos_s):
    tm = _row_tile(h.shape[0])
    tn = 4 * DIFF_DH
    n_groups = tn // DIFF_DH
    q_scale = DIFF_DH ** -0.5

    def q_epilogue(acc, extra_refs, out_refs):
        gain, c, s_lo, s_hi, group_ones = (r[...] for r in extra_refs)
        for gi, rot in enumerate(_norm_rope_groups(acc, gain, c, s_lo, s_hi, group_ones, n_groups)):
            out_refs[0][:, gi * DIFF_DH:(gi + 1) * DIFF_DH] = (rot * q_scale).astype(out_refs[0].dtype)

    N = w_q[0].shape[2]
    (q,), (q_s,) = ws_matmul(h, hs, w_q, (0,), N // tn, tm, tn, q_epilogue, [(N, BF16)],
                             extras=_diff_norm_rope_extras(g_q_norm, pos, pos_s, tn), name="diff_q_proj")
    return q, q_s


def _lambda_value(lam_ref, lam_init):
    l = lam_ref[...]
    a = jnp.sum(l[0:1, :] * l[1:2, :], axis=-1, keepdims=True)
    b = jnp.sum(l[2:3, :] * l[3:4, :], axis=-1, keepdims=True)
    return jnp.exp(a) - jnp.exp(b) + lam_init


def _sub_norm(o, g_sub, lam_init):
    y = o * lax.rsqrt(jnp.mean(o * o, axis=-1, keepdims=True) + EPS) * g_sub
    return y * (1.0 - lam_init)


def _diff_attn_body(q_ref, k_ref, v_ref, lam_ref, gsub_ref, o_ref, m_ref, l_ref, acc_ref, *, tq, lam_init):
    qi = pl.program_id(2)
    tk = tq
    DH, DV = DIFF_DH, DIFF_DV
    m_ref[...] = jnp.full(m_ref.shape, -jnp.inf, F32)
    l_ref[...] = jnp.zeros(l_ref.shape, F32)
    acc_ref[...] = jnp.zeros(acc_ref.shape, F32)

    def update(j, masked):
        ksl = pl.ds(pl.multiple_of(j * tk, tk), tk)
        kblk = k_ref[ksl, :]
        vblk = v_ref[ksl, :]
        for c in range(2):
            s = lax.dot_general(q_ref[:, c * DH:(c + 1) * DH], kblk[:, c * DH:(c + 1) * DH],
                                (((1,), (1,)), ((), ())), preferred_element_type=F32)
            if masked:
                rr = lax.broadcasted_iota(jnp.int32, (tq, tk), 0)
                cc = lax.broadcasted_iota(jnp.int32, (tq, tk), 1)
                s = jnp.where(cc <= rr, s, -jnp.inf)
            m_prev = m_ref[c]
            m_new = jnp.maximum(m_prev, jnp.max(s, axis=-1, keepdims=True))
            corr = jnp.exp(m_prev - m_new)
            p = jnp.exp(s - jnp.concatenate([m_new] * (tk // LANES), axis=1))
            p_lanes = p[:, 0:LANES]
            for t in range(1, tk // LANES):
                p_lanes = p_lanes + p[:, t * LANES:(t + 1) * LANES]
            l_ref[c] = l_ref[c] * corr + p_lanes
            acc_ref[c] = (acc_ref[c] * jnp.concatenate([corr] * (DV // LANES), axis=1)
                          + jnp.dot(p.astype(BF16), vblk, preferred_element_type=F32))
            m_ref[c] = m_new

    def body(j, carry):
        update(j, False)
        return carry

    lax.fori_loop(0, qi, body, 0)
    update(qi, True)

    lam = _lambda_value(lam_ref, lam_init)
    l0 = jnp.sum(l_ref[0], axis=-1, keepdims=True)
    l1 = jnp.sum(l_ref[1], axis=-1, keepdims=True)
    o = acc_ref[0] / l0 - lam * (acc_ref[1] / l1)
    o_ref[...] = _sub_norm(o, gsub_ref[...], lam_init).astype(o_ref.dtype)


def diff_attn_prompt(q, k, v, lam_rows, g_sub, lam_init, batch, seq, tq=512):
    M, D = q.shape
    H, DV = DIFF_HEADS, DIFF_DV
    tq = min(seq, tq)
    nq = seq // tq
    body = functools.partial(_diff_attn_body, tq=tq, lam_init=lam_init)
    return pl.pallas_call(
        body,
        grid=(batch, H, nq),
        in_specs=[pl.BlockSpec((tq, DV), lambda b, h, i: (b * nq + i, h)),
                  pl.BlockSpec((seq, DV), lambda b, h, i: (b, h)),
                  pl.BlockSpec((seq, DV), lambda b, h, i: (b, h)),
                  pl.BlockSpec((4, DIFF_DH), lambda b, h, i: (0, 0)),
                  pl.BlockSpec((1, DV), lambda b, h, i: (0, 0))],
        out_specs=pl.BlockSpec((tq, DV), lambda b, h, i: (b * nq + i, h)),
        out_shape=jax.ShapeDtypeStruct((M, D), BF16),
        scratch_shapes=[pltpu.VMEM((2, tq, LANES), F32),
                        pltpu.VMEM((2, tq, LANES), F32),
                        pltpu.VMEM((2, tq, DV), F32)],
        compiler_params=_params(3),
        name="diff_attn_prompt",
    )(q, k, v, lam_rows, g_sub.reshape(1, DV).astype(F32))


def _diff_decode_body(pt_ref, qt_ref, *refs, pages_per_step, lam_init):
    k_refs = refs[:pages_per_step]
    v_refs = refs[pages_per_step:2 * pages_per_step]
    knew_ref, vnew_ref, lam_ref, gsub_ref, o_ref, m_ref, l_ref, acc_ref = refs[2 * pages_per_step:]
    step = pl.program_id(1)
    n_steps = pl.num_programs(1)
    H = DIFF_HEADS
    NQ = 2 * H

    @pl.when(step == 0)
    def _():
        m_ref[...] = jnp.full(m_ref.shape, -jnp.inf, F32)
        l_ref[...] = jnp.zeros(l_ref.shape, F32)
        acc_ref[...] = jnp.zeros(acc_ref.shape, F32)

    qt = qt_ref[0]

    def update(kb, vb, n_real_rows):
        n_rows = kb.shape[0]
        st = lax.dot_general(qt, kb, (((1,), (1,)), ((), ())), preferred_element_type=F32)
        rr = lax.broadcasted_iota(jnp.int32, (NQ, n_rows), 0)
        cc = lax.broadcasted_iota(jnp.int32, (NQ, n_rows), 1)
        valid = (cc % H) == (rr % H)
        if n_real_rows < n_rows:
            valid = valid & (cc < n_real_rows)
        st = jnp.where(valid, st, -jnp.inf)
        m_prev = m_ref[...]
        m_new = jnp.maximum(m_prev, jnp.max(st, axis=-1, keepdims=True))
        corr = jnp.exp(m_prev - m_new)
        p = jnp.exp(st - m_new[:, :1])
        l_ref[...] = l_ref[...] * corr + jnp.sum(p, axis=-1, keepdims=True)
        acc_ref[...] = acc_ref[...] * corr[:, :1] + jnp.dot(p.astype(BF16), vb, preferred_element_type=F32)
        m_ref[...] = m_new

    for r in range(pages_per_step):
        update(k_refs[r][0].astype(BF16), v_refs[r][0].astype(BF16), PAGE_SIZE * H)

    @pl.when(step == n_steps - 1)
    def _():
        pad = jnp.zeros((LANES - H, knew_ref.shape[2]), F32)
        kn = jnp.concatenate([knew_ref[0], pad], axis=0).astype(BF16)
        vn = jnp.concatenate([vnew_ref[0], pad], axis=0).astype(BF16)
        update(kn, vn, H)
        lam = _lambda_value(lam_ref, lam_init)
        acc = acc_ref[...]
        l = l_ref[...]
        o = acc[0:H] / l[0:H, :1] - lam * (acc[H:NQ] / l[H:NQ, :1])
        o_ref[0] = _sub_norm(o, gsub_ref[...], lam_init)


def diff_attn_decode(q, k_new, v_new, cache_k, cache_v, page_table, lam_rows, g_sub, lam_init):
    B, D = q.shape
    H, DH, DV = DIFF_HEADS, DIFF_DH, DIFF_DV
    n_pool = cache_k.shape[0]
    n_pages = page_table.shape[1]
    pages_per_step = 2
    n_steps = n_pages // pages_per_step
    rows = PAGE_SIZE * H
    ck = cache_k.reshape(n_pool, rows, 2 * DH)
    cv = cache_v.reshape(n_pool, rows, DV)
    q4 = q.reshape(B, H, 2, DH)
    zero = jnp.zeros((B, H, DH), q.dtype)
    qt = jnp.concatenate([jnp.concatenate([q4[:, :, 0], zero], axis=-1),
                          jnp.concatenate([zero, q4[:, :, 1]], axis=-1)], axis=1).astype(BF16)

    def page_spec(r, width):
        return pl.BlockSpec((1, rows, width), lambda b, s, pt: (pt[b, s * pages_per_step + r], 0, 0))

    body = functools.partial(_diff_decode_body, pages_per_step=pages_per_step, lam_init=lam_init)
    grid_spec = pltpu.PrefetchScalarGridSpec(
        num_scalar_prefetch=1,
        grid=(B, n_steps),
        in_specs=[pl.BlockSpec((1, 2 * H, 2 * DH), lambda b, s, pt: (b, 0, 0))]
                 + [page_spec(r, 2 * DH) for r in range(pages_per_step)]
                 + [page_spec(r, DV) for r in range(pages_per_step)]
                 + [pl.BlockSpec((1, H, 2 * DH), lambda b, s, pt: (b, 0, 0)),
                    pl.BlockSpec((1, H, DV), lambda b, s, pt: (b, 0, 0)),
                    pl.BlockSpec((4, DH), lambda b, s, pt: (0, 0)),
                    pl.BlockSpec((1, DV), lambda b, s, pt: (0, 0))],
        out_specs=pl.BlockSpec((1, H, DV), lambda b, s, pt: (b, 0, 0)),
        scratch_shapes=[pltpu.VMEM((2 * H, LANES), F32),
                        pltpu.VMEM((2 * H, LANES), F32),
                        pltpu.VMEM((2 * H, DV), F32)],
    )
    out = pl.pallas_call(
        body,
        grid_spec=grid_spec,
        out_shape=jax.ShapeDtypeStruct((B, H, DV), F32),
        compiler_params=_params(2),
        name="diff_attn_decode",
    )(page_table, qt, *([ck] * pages_per_step), *([cv] * pages_per_step),
      k_new.reshape(B, H, 2 * DH), v_new.reshape(B, H, DV), lam_rows, g_sub.reshape(1, DV).astype(F32))
    return out.reshape(B, H * DV)


def kernel(x_prompt, x_sample, state_ret, cache_k, cache_v, page_table, g_ffn1, w_ffn1_gu, w_ffn1_down, g_mix, g_ffn2, w_ffn2_gu, w_ffn2_down, w_ret_qkvg, g_ret_gn, w_ret_o, g_kv, w_kv, g_k_norm, w_diff_q, g_q_norm, lambda_q1, lambda_k1, lambda_q2, lambda_k2, g_sub, w_diff_o):
    B, S, D = x_prompt.shape
    Bd, Sd, _ = x_sample.shape
    pos_p = jnp.arange(S)
    pos_s = jnp.broadcast_to(PAST_LEN + jnp.arange(Sd), (Bd * Sd,))
    xp = x_prompt.reshape(B * S, D)
    xs = x_sample.reshape(Bd * Sd, D)
    ret_p, ret_s = [], []
    kp32 = vp32 = ks32 = vs32 = kp = vp = None
    for i in range(DEPTH):
        if i == N_A_LAYERS:
            (kp32, kp, vp32, vp), (ks32, vs32) = shared_kv(xp, xs, g_kv, (w_kv[None], 0), g_k_norm, pos_p, pos_s)
        xp, xs = swiglu_half_step(xp, xs, g_ffn1[i], (w_ffn1_gu, i), (w_ffn1_down, i))
        if i < N_A_LAYERS:
            a = i
            (qk, v, g), (qk_s, v_s, g_s) = ret_project(
                rmsnorm_cast(xp, g_mix[i], BF16), rmsnorm_cast(xs, g_mix[i], F32), (w_ret_qkvg, a), pos_p, pos_s)
            y, s_fin = retention_prompt(qk, v, g, g_ret_gn[a], B, S)
            y_s, s_new = retention_step(qk_s, v_s, g_s, g_ret_gn[a], state_ret[a])
            xp, xs = matmul_residual(y, y_s, (w_ret_o, a), xp, xs, 1.0, name="ret_out_proj")
            ret_p.append(s_fin)
            ret_s.append(s_new)
        else:
            bidx = i - N_A_LAYERS
            lam_init = 0.8 - 0.6 * math.exp(-0.3 * i)
            lam_rows = jnp.stack([lambda_q1[bidx], lambda_k1[bidx],
                                  lambda_q2[bidx], lambda_k2[bidx]]).astype(F32)
            q, q_s = diff_query(rmsnorm_cast(xp, g_mix[i], BF16), rmsnorm_cast(xs, g_mix[i], F32),
                                (w_diff_q, bidx), g_q_norm[bidx], pos_p, pos_s)
            o = diff_attn_prompt(q, kp, vp, lam_rows, g_sub[bidx], lam_init, B, S)
            o_s = diff_attn_decode(q_s, ks32, vs32, cache_k, cache_v, page_table, lam_rows, g_sub[bidx], lam_init)
            xp, xs = matmul_residual(o, o_s, (w_diff_o, bidx), xp, xs, 1.0, name="diff_out_proj")
        xp, xs = swiglu_half_step(xp, xs, g_ffn2[i], (w_ffn2_gu, i), (w_ffn2_down, i))
    return (xp.reshape(B, S, D), xs.reshape(Bd, Sd, D),
            jnp.stack(ret_p), jnp.stack(ret_s),
            kp32.reshape(B, S, DIFF_HEADS, 2 * DIFF_DH), vp32.reshape(B, S, DIFF_HEADS, DIFF_DV),
            ks32.reshape(Bd, Sd, DIFF_HEADS, 2 * DIFF_DH), vs32.reshape(Bd, Sd, DIFF_HEADS, DIFF_DV))
```

```python
import functools
import math

import jax
import jax.numpy as jnp
from jax import lax
from jax.experimental import pallas as pl
from jax.experimental.pallas import tpu as pltpu

F32 = jnp.float32
BF16 = jnp.bfloat16

D_MODEL = 4096
SEQ = 2048
DEPTH = 2
PAST_LEN = 16384
PAGE_SIZE = 128
N_A_LAYERS = DEPTH // 2
RET_HEADS = 16
RET_DK = D_MODEL // RET_HEADS
RET_CHUNK = 128
RET_THETA = 10000.0
DIFF_HEADS = 16
DIFF_DH = D_MODEL // (2 * DIFF_HEADS)
DIFF_DV = 2 * DIFF_DH
ROT_DIM = DIFF_DH // 4
ROPE_THETA = 500000.0
D_FF = 256 * ((8 * D_MODEL // 3 + 255) // 256)
EPS = 1e-6

LANES = 128
VMEM_LIMIT_BYTES = 56 * 1024 * 1024


def _params(n_grid_dims):
    return pltpu.CompilerParams(
        dimension_semantics=("arbitrary",) * n_grid_dims,
        vmem_limit_bytes=VMEM_LIMIT_BYTES)


def _silu(x):
    return x * jax.nn.sigmoid(x)


def _rmsnorm_body(x_ref, g_ref, o_ref):
    x = x_ref[...].astype(F32)
    y = x * lax.rsqrt(jnp.mean(x * x, axis=-1, keepdims=True) + EPS)
    o_ref[...] = (y * g_ref[...]).astype(o_ref.dtype)


def rmsnorm_cast(x, g, out_dtype):
    M, D = x.shape
    tm = min(M, 512)
    return pl.pallas_call(
        _rmsnorm_body,
        grid=(M // tm,),
        in_specs=[pl.BlockSpec((tm, D), lambda i: (i, 0)),
                  pl.BlockSpec((1, D), lambda i: (0, 0))],
        out_specs=pl.BlockSpec((tm, D), lambda i: (i, 0)),
        out_shape=jax.ShapeDtypeStruct((M, D), out_dtype),
        compiler_params=_params(1),
        name="rmsnorm_cast",
    )(x, g.reshape(1, D).astype(F32))


def _rs_matmul_body(*refs, n_w, n_extra, n_out, n_row_tiles, col, epilogue):
    it = iter(refs)
    x_ref, xs_ref = next(it), next(it)
    w_refs = [next(it) for _ in range(n_w)]
    extra_refs = [next(it) for _ in range(n_extra)]
    extra_s_refs = [next(it) for _ in range(n_extra)]
    out_refs = [next(it) for _ in range(n_out)]
    out_s_refs = [next(it) for _ in range(n_out)]
    col_tile = col(pl.program_id(0), pl.program_id(1))

    def weights_bf16():
        return jnp.concatenate([w_ref[...].astype(BF16) for w_ref in w_refs], axis=1)

    acc = jnp.dot(x_ref[...].astype(BF16), weights_bf16(), preferred_element_type=F32)
    epilogue(acc, extra_refs, out_refs, col_tile)

    @pl.when(pl.program_id(0) == n_row_tiles - 1)
    def _():
        acc_s = jnp.dot(xs_ref[...].astype(BF16), weights_bf16(), preferred_element_type=F32)
        epilogue(acc_s, extra_s_refs, out_s_refs, col_tile)


def ws_matmul(x, xs, w, col_block_offsets, n_tiles, tm, tnw, epilogue, outs,
              extras=(), k_block=0, tk=None, name="rs_matmul"):
    M, Ms = x.shape[0], xs.shape[0]
    w_all, layer = w
    tk = x.shape[1] if tk is None else tk
    n_w = len(col_block_offsets)
    nrt = M // tm
    col = lambda i, j: jnp.where(i % 2 == 0, j, n_tiles - 1 - j)
    first_col_last_sweep = 0 if (nrt - 1) % 2 == 0 else n_tiles - 1
    col_s = lambda i, j: jnp.where(i == nrt - 1, col(i, j), first_col_last_sweep)
    in_specs = [pl.BlockSpec((tm, tk), lambda i, j: (i, k_block), pipeline_mode=pl.Buffered(1)),
                pl.BlockSpec((Ms, tk), lambda i, j: (0, k_block))]
    for off in col_block_offsets:
        in_specs.append(pl.BlockSpec((None, tk, tnw), lambda i, j, off=off: (layer, k_block, off + col(i, j))))
    specs_p, specs_s, args_p, args_s = [], [], [], []
    for kind, *arrs in extras:
        if kind == "tile":
            a, a_s = arrs
            wd = a.shape[1] // n_tiles
            specs_p.append(pl.BlockSpec((tm, wd), lambda i, j: (i, col(i, j))))
            specs_s.append(pl.BlockSpec((Ms, wd), lambda i, j: (0, col_s(i, j))))
        elif kind == "rows":
            a, a_s = arrs
            npb = max(a.shape[0] // tm, 1)
            specs_p.append(pl.BlockSpec((tm, a.shape[1]), lambda i, j, npb=npb: (i % npb, 0)))
            specs_s.append(pl.BlockSpec((Ms, a.shape[1]), lambda i, j: (0, 0)))
        else:
            a, = arrs
            a_s = a
            specs_p.append(pl.BlockSpec(a.shape, lambda i, j: (0, 0)))
            specs_s.append(pl.BlockSpec(a.shape, lambda i, j: (0, 0)))
        args_p.append(a)
        args_s.append(a_s)
    out_specs_p, out_specs_s, shapes_p, shapes_s = [], [], [], []
    for n_cols, dtype in outs:
        wd = n_cols // n_tiles
        out_specs_p.append(pl.BlockSpec((tm, wd), lambda i, j: (i, col(i, j))))
        out_specs_s.append(pl.BlockSpec((Ms, wd), lambda i, j: (0, col_s(i, j))))
        shapes_p.append(jax.ShapeDtypeStruct((M, n_cols), dtype))
        shapes_s.append(jax.ShapeDtypeStruct((Ms, n_cols), F32))
    body = functools.partial(_rs_matmul_body, n_w=n_w, n_extra=len(extras), n_out=len(outs),
                             n_row_tiles=nrt, col=col, epilogue=epilogue)
    res = pl.pallas_call(
        body,
        grid=(nrt, n_tiles),
        in_specs=in_specs + specs_p + specs_s,
        out_specs=out_specs_p + out_specs_s,
        out_shape=shapes_p + shapes_s,
        compiler_params=_params(2),
        name=name,
    )(x, xs, *([w_all] * n_w), *args_p, *args_s)
    n = len(outs)
    return res[:n], res[n:]


ROWS = 1024
ROWS_WIDE = 2048


def ffn_gate_up(h, hs, w_gu):
    d_ff = w_gu[0].shape[2] // 2
    tnw = 256
    n_tiles = d_ff // tnw

    def epilogue(acc, extra_refs, out_refs, col_tile):
        gate, up = acc[:, :tnw], acc[:, tnw:]
        out_refs[0][...] = (_silu(gate) * up).astype(out_refs[0].dtype)

    (u,), (us,) = ws_matmul(h, hs, w_gu, (0, n_tiles), n_tiles, min(h.shape[0], ROWS_WIDE), tnw, epilogue,
                            [(d_ff, BF16)], name="ffn_gate_up")
    return u, us


def matmul_residual(a, a_s, w, res, res_s, alpha, k_splits=1, name="matmul_residual"):
    M, K = a.shape
    N = w[0].shape[2]
    tm = min(M, ROWS)
    tn = 512
    tk = K // k_splits

    def epilogue(acc, extra_refs, out_refs, col_tile):
        out_refs[0][...] = extra_refs[0][...] + alpha * acc

    for kb in range(k_splits):
        (res,), (res_s,) = ws_matmul(a, a_s, w, (0,), N // tn, tm, tn, epilogue, [(N, F32)],
                                     extras=[("tile", res, res_s)], k_block=kb, tk=tk, name=name)
    return res, res_s


def swiglu_half_step(x, xs, g, w_gu, w_down):
    h = rmsnorm_cast(x, g, BF16)
    hs = rmsnorm_cast(xs, g, F32)
    u, us = ffn_gate_up(h, hs, w_gu)
    return matmul_residual(u, us, w_down, x, xs, 0.5, k_splits=2, name="ffn_down")


def _rope_tables_full(pos, rot_dim, theta):
    half = rot_dim // 2
    inv = 1.0 / (theta ** (jnp.arange(half, dtype=F32) / half))
    ang = pos.astype(F32)[:, None] * inv[None, :]
    return jnp.cos(ang), jnp.sin(ang)


def _rope_tables_partial(pos, rot_dim, theta, width):
    half = rot_dim // 2
    cos, sin = _rope_tables_full(pos, rot_dim, theta)
    P = pos.shape[0]
    ones = jnp.ones((P, width - rot_dim), F32)
    zeros_rest = jnp.zeros((P, width - rot_dim), F32)
    zeros_half = jnp.zeros((P, half), F32)
    c = jnp.concatenate([cos, cos, ones], axis=1)
    s_lo = jnp.concatenate([-sin, zeros_half, zeros_rest], axis=1)
    s_hi = jnp.concatenate([zeros_half, sin, zeros_rest], axis=1)
    return c, s_lo, s_hi


def ret_project(h, hs, w_qkvg, pos, pos_s):
    M, D = h.shape
    tm = min(M, ROWS_WIDE)
    tn = 2 * RET_DK
    half = RET_DK // 2
    cos, sin = _rope_tables_full(pos, RET_DK, RET_THETA)
    cos_s, sin_s = _rope_tables_full(pos_s, RET_DK, RET_THETA)
    k_scale = RET_DK ** -0.5
    n_head_tiles = D // tn

    def rope_epilogue(acc, extra_refs, out_refs, col_tile):
        c = extra_refs[0][...]
        s = extra_refs[1][...]
        is_k = col_tile >= n_head_tiles
        scale = jnp.where(is_k, k_scale, 1.0).astype(F32)
        for hh in range(tn // RET_DK):
            x1 = acc[:, hh * RET_DK:hh * RET_DK + half]
            x2 = acc[:, hh * RET_DK + half:(hh + 1) * RET_DK]
            out_refs[0][:, hh * RET_DK:hh * RET_DK + half] = (
                (x1 * c - x2 * s) * scale).astype(out_refs[0].dtype)
            out_refs[0][:, hh * RET_DK + half:(hh + 1) * RET_DK] = (
                (x2 * c + x1 * s) * scale).astype(out_refs[0].dtype)

    def plain_epilogue(acc, extra_refs, out_refs, col_tile):
        out_refs[0][...] = acc.astype(out_refs[0].dtype)

    (qk,), (qk_s,) = ws_matmul(h, hs, w_qkvg, (0,), 2 * n_head_tiles, tm, tn, rope_epilogue,
                               [(2 * D, BF16)], extras=[("rows", cos, cos_s), ("rows", sin, sin_s)],
                               name="ret_qk_proj")
    (v,), (v_s,) = ws_matmul(h, hs, w_qkvg, (2 * n_head_tiles,), n_head_tiles, tm, tn, plain_epilogue,
                             [(D, BF16)], name="ret_v_proj")
    (g,), (g_s,) = ws_matmul(h, hs, w_qkvg, (3 * n_head_tiles,), n_head_tiles, tm, tn, plain_epilogue,
                             [(D, F32)], name="ret_g_proj")
    return (qk, v, g), (qk_s, v_s, g_s)


def _log_gamma_rows():
    lg = jnp.log1p(-(2.0 ** (-5.0 - jnp.arange(RET_HEADS, dtype=F32))))
    return jnp.broadcast_to(lg[:, None, None], (RET_HEADS, 1, RET_DK))


def _group_norm_gate(o, gn, g):
    mu = jnp.mean(o, axis=-1, keepdims=True)
    d = o - mu
    var = jnp.mean(d * d, axis=-1, keepdims=True)
    y = d * lax.rsqrt(var + EPS) * gn
    return _silu(g) * y


RET_HEADS_PER_STEP = 2


def _ret_prompt_body(q_ref, k_ref, v_ref, g_ref, lg_ref, gn_ref, y_ref, s_ref, state_ref, *, n_chunks):
    L, DK = RET_CHUNK, RET_DK
    r = lax.broadcasted_iota(jnp.int32, (L, L), 0)
    c = lax.broadcasted_iota(jnp.int32, (L, L), 1)
    diff = (r - c).astype(F32)
    rows = lax.broadcasted_iota(jnp.int32, (L, DK), 0).astype(F32)
    decays = []
    for hh in range(RET_HEADS_PER_STEP):
        lg = lg_ref[hh]
        decays.append((jnp.where(diff >= 0, jnp.exp(jnp.maximum(diff, 0.0) * lg[:, :L]), 0.0),
                       jnp.exp((rows + 1.0) * lg),
                       jnp.exp((L - 1.0 - rows) * lg),
                       jnp.exp(float(L) * lg)))
    state_ref[...] = jnp.zeros_like(state_ref)

    def chunk(ci, carry):
        sl = pl.ds(pl.multiple_of(ci * L, L), L)
        for hh in range(RET_HEADS_PER_STEP):
            dmat, q_decay, k_decay, chunk_decay = decays[hh]
            hs = slice(hh * DK, (hh + 1) * DK)
            q = q_ref[sl, hs]
            k = k_ref[sl, hs]
            v = v_ref[sl, hs]
            state = state_ref[hh]
            scores = lax.dot_general(q, k, (((1,), (1,)), ((), ())), preferred_element_type=F32) * dmat
            inner = jnp.dot(scores.astype(BF16), v, preferred_element_type=F32)
            cross = jnp.dot(q, state.astype(BF16), preferred_element_type=F32) * q_decay
            o = inner + cross
            kd = (k.astype(F32) * k_decay).astype(BF16)
            state_ref[hh] = chunk_decay * state + lax.dot_general(
                kd, v, (((0,), (0,)), ((), ())), preferred_element_type=F32)
            y_ref[sl, hs] = _group_norm_gate(o, gn_ref[:, hs], g_ref[sl, hs]).astype(y_ref.dtype)
        return carry

    lax.fori_loop(0, n_chunks, chunk, 0)
    s_ref[0] = state_ref[...]


def retention_prompt(qk, v, g, gn_gain, batch, seq):
    M, D = v.shape
    H, DK, HPS = RET_HEADS, RET_DK, RET_HEADS_PER_STEP
    W = HPS * DK
    body = functools.partial(_ret_prompt_body, n_chunks=seq // RET_CHUNK)
    y, s_fin = pl.pallas_call(
        body,
        grid=(batch, H // HPS),
        in_specs=[pl.BlockSpec((seq, W), lambda b, h: (b, h)),
                  pl.BlockSpec((seq, W), lambda b, h: (b, H // HPS + h)),
                  pl.BlockSpec((seq, W), lambda b, h: (b, h)),
                  pl.BlockSpec((seq, W), lambda b, h: (b, h)),
                  pl.BlockSpec((HPS, 1, DK), lambda b, h: (h, 0, 0)),
                  pl.BlockSpec((1, W), lambda b, h: (0, h))],
        out_specs=[pl.BlockSpec((seq, W), lambda b, h: (b, h)),
                   pl.BlockSpec((1, HPS, DK, DK), lambda b, h: (b, h, 0, 0))],
        out_shape=[jax.ShapeDtypeStruct((M, D), BF16),
                   jax.ShapeDtypeStruct((batch, H, DK, DK), F32)],
        scratch_shapes=[pltpu.VMEM((HPS, DK, DK), F32)],
        compiler_params=_params(2),
        name="retention_prompt",
    )(qk, qk, v, g, _log_gamma_rows(), gn_gain.reshape(1, D).astype(F32))
    return y, s_fin


def _ret_step_body(q_ref, k_ref, v_ref, g_ref, lg_ref, gn_ref, s_in_ref, y_ref, s_out_ref, *, batch):
    lg = lg_ref[0]
    d0 = jnp.exp(0.0 * lg)
    q_decay = jnp.exp(1.0 * lg)
    k_decay = jnp.exp(0.0 * lg)
    step_decay = jnp.exp(1.0 * lg)
    rnd = lambda t: t.astype(BF16).astype(F32)
    qf, kf, vf = rnd(q_ref[...]), rnd(k_ref[...]), rnd(v_ref[...])
    row = lax.broadcasted_iota(jnp.int32, qf.shape, 0)
    scores = jnp.sum(qf * kf, axis=-1, keepdims=True) * d0[:, :1]
    inner = rnd(scores) * vf
    kd = rnd(kf * k_decay)
    cross = jnp.zeros(qf.shape, F32)
    for b in range(batch):
        state = s_in_ref[b, 0].astype(F32)
        q_only_b = jnp.where(row == b, qf, 0.0)
        k_only_b = jnp.where(row == b, kd, 0.0)
        cross = cross + jnp.dot(q_only_b, rnd(state), preferred_element_type=F32)
        s_out_ref[b, 0] = step_decay * state + lax.dot_general(
            k_only_b, vf, (((0,), (0,)), ((), ())), preferred_element_type=F32)
    o = inner + cross * q_decay
    y_ref[...] = _group_norm_gate(o, gn_ref[...], g_ref[...]).astype(y_ref.dtype)


def retention_step(qk, v, g, gn_gain, state):
    B, D = v.shape
    H, DK = RET_HEADS, RET_DK
    body = functools.partial(_ret_step_body, batch=B)
    y, s_new = pl.pallas_call(
        body,
        grid=(H,),
        in_specs=[pl.BlockSpec((B, DK), lambda h: (0, h)),
                  pl.BlockSpec((B, DK), lambda h: (0, H + h)),
                  pl.BlockSpec((B, DK), lambda h: (0, h)),
                  pl.BlockSpec((B, DK), lambda h: (0, h)),
                  pl.BlockSpec((1, 1, DK), lambda h: (h, 0, 0)),
                  pl.BlockSpec((1, DK), lambda h: (0, h)),
                  pl.BlockSpec((B, 1, DK, DK), lambda h: (0, h, 0, 0))],
        out_specs=[pl.BlockSpec((B, DK), lambda h: (0, h)),
                   pl.BlockSpec((B, 1, DK, DK), lambda h: (0, h, 0, 0))],
        out_shape=[jax.ShapeDtypeStruct((B, D), F32),
                   jax.ShapeDtypeStruct((B, H, DK, DK), F32)],
        compiler_params=_params(1),
        name="retention_step",
    )(qk, qk, v, g, _log_gamma_rows(), gn_gain.reshape(1, D).astype(F32), state)
    return y, s_new


def _norm_rope_groups(acc, gain, c, s_lo, s_hi, group_ones, n_groups):
    half = ROT_DIM // 2
    sq = acc * acc
    hi = sq.astype(BF16)
    lo = (sq - hi.astype(F32)).astype(BF16)
    ssq = (jnp.dot(hi, group_ones, preferred_element_type=F32)
           + jnp.dot(lo, group_ones, preferred_element_type=F32))
    inv = lax.rsqrt(ssq * (1.0 / DIFF_DH) + EPS)
    outs = []
    for gi in range(n_groups):
        sl = slice(gi * DIFF_DH, (gi + 1) * DIFF_DH)
        yg = acc[:, sl] * inv[:, sl] * gain
        rot = (yg * c + pltpu.roll(yg, DIFF_DH - half, 1) * s_lo + pltpu.roll(yg, half, 1) * s_hi)
        outs.append(rot)
    return outs


def _diff_norm_rope_extras(gain, pos, pos_s, tn):
    tabs = _rope_tables_partial(pos, ROT_DIM, ROPE_THETA, DIFF_DH)
    tabs_s = _rope_tables_partial(pos_s, ROT_DIM, ROPE_THETA, DIFF_DH)
    lane_group = jnp.arange(tn) // DIFF_DH
    group_ones = (lane_group[:, None] == lane_group[None, :]).astype(BF16)
    return ([("const", gain.reshape(1, DIFF_DH).astype(F32))]
            + [("rows", a, a_s) for a, a_s in zip(tabs, tabs_s)]
            + [("const", group_ones)])


def shared_kv(x, xs, g_kv, w_kv, g_k_norm, pos, pos_s):
    h = rmsnorm_cast(x, g_kv, BF16)
    hs = rmsnorm_cast(xs, g_kv, F32)
    tm = min(x.shape[0], ROWS)
    tn = 4 * DIFF_DH
    n_groups = tn // DIFF_DH
    nk = (DIFF_HEADS * 2 * DIFF_DH) // tn
    nv = (DIFF_HEADS * DIFF_DV) // tn

    def k_epilogue(acc, extra_refs, out_refs, col_tile):
        gain, c, s_lo, s_hi, group_ones = (r[...] for r in extra_refs)
        for gi, rot in enumerate(_norm_rope_groups(acc, gain, c, s_lo, s_hi, group_ones, n_groups)):
            out_refs[0][:, gi * DIFF_DH:(gi + 1) * DIFF_DH] = rot
            out_refs[1][:, gi * DIFF_DH:(gi + 1) * DIFF_DH] = rot.astype(out_refs[1].dtype)

    def v_epilogue(acc, extra_refs, out_refs, col_tile):
        out_refs[0][...] = acc
        out_refs[1][...] = acc.astype(out_refs[1].dtype)

    (k32, kbf), (k32_s, _) = ws_matmul(h, hs, w_kv, (0,), nk, tm, tn, k_epilogue,
                                       [(nk * tn, F32), (nk * tn, BF16)],
                                       extras=_diff_norm_rope_extras(g_k_norm, pos, pos_s, tn),
                                       name="shared_k_proj")
    (v32, vbf), (v32_s, _) = ws_matmul(h, hs, w_kv, (nk,), nv, tm, tn, v_epilogue,
                                       [(nv * tn, F32), (nv * tn, BF16)], name="shared_v_proj")
    return (k32, kbf, v32, vbf), (k32_s, v32_s)


def diff_query(h, hs, w_q, g_q_norm, pos, pos_s):
    tm = min(h.shape[0], ROWS)
    tn = 4 * DIFF_DH
    n_groups = tn // DIFF_DH
    q_scale = DIFF_DH ** -0.5

    def q_epilogue(acc, extra_refs, out_refs, col_tile):
        gain, c, s_lo, s_hi, group_ones = (r[...] for r in extra_refs)
        for gi, rot in enumerate(_norm_rope_groups(acc, gain, c, s_lo, s_hi, group_ones, n_groups)):
            out_refs[0][:, gi * DIFF_DH:(gi + 1) * DIFF_DH] = (rot * q_scale).astype(out_refs[0].dtype)

    N = w_q[0].shape[2]
    (q,), (q_s,) = ws_matmul(h, hs, w_q, (0,), N // tn, tm, tn, q_epilogue, [(N, BF16)],
                             extras=_diff_norm_rope_extras(g_q_norm, pos, pos_s, tn), name="diff_q_proj")
    return q, q_s


def _lambda_value(lam_ref, lam_init):
    l = lam_ref[...]
    a = jnp.sum(l[0:1, :] * l[1:2, :], axis=-1, keepdims=True)
    b = jnp.sum(l[2:3, :] * l[3:4, :], axis=-1, keepdims=True)
    return jnp.exp(a) - jnp.exp(b) + lam_init


def _sub_norm(o, g_sub, lam_init):
    y = o * lax.rsqrt(jnp.mean(o * o, axis=-1, keepdims=True) + EPS) * g_sub
    return y * (1.0 - lam_init)


def _diff_attn_body(q_ref, k_ref, v_ref, lam_ref, gsub_ref, o_ref, m_ref, l_ref, acc_ref, *, tq, lam_init):
    qi = pl.program_id(2)
    tk = tq
    DH, DV = DIFF_DH, DIFF_DV
    m_ref[...] = jnp.full(m_ref.shape, -jnp.inf, F32)
    l_ref[...] = jnp.zeros(l_ref.shape, F32)
    acc_ref[...] = jnp.zeros(acc_ref.shape, F32)

    def update(j, masked):
        ksl = pl.ds(pl.multiple_of(j * tk, tk), tk)
        kblk = k_ref[ksl, :]
        vblk = v_ref[ksl, :]
        for c in range(2):
            s = lax.dot_general(q_ref[:, c * DH:(c + 1) * DH], kblk[:, c * DH:(c + 1) * DH],
                                (((1,), (1,)), ((), ())), preferred_element_type=F32)
            if masked:
                rr = lax.broadcasted_iota(jnp.int32, (tq, tk), 0)
                cc = lax.broadcasted_iota(jnp.int32, (tq, tk), 1)
                s = jnp.where(cc <= rr, s, -jnp.inf)
            m_prev = m_ref[c]
            m_new = jnp.maximum(m_prev, jnp.max(s, axis=-1, keepdims=True))
            corr = jnp.exp(m_prev - m_new)
            p = jnp.exp(s - jnp.concatenate([m_new] * (tk // LANES), axis=1))
            p_lanes = p[:, 0:LANES]
            for t in range(1, tk // LANES):
                p_lanes = p_lanes + p[:, t * LANES:(t + 1) * LANES]
            l_ref[c] = l_ref[c] * corr + p_lanes
            acc_ref[c] = (acc_ref[c] * jnp.concatenate([corr] * (DV // LANES), axis=1)
                          + jnp.dot(p.astype(BF16), vblk, preferred_element_type=F32))
            m_ref[c] = m_new

    def body(j, carry):
        update(j, False)
        return carry

    lax.fori_loop(0, qi, body, 0)
    update(qi, True)

    lam = _lambda_value(lam_ref, lam_init)
    l0 = jnp.sum(l_ref[0], axis=-1, keepdims=True)
    l1 = jnp.sum(l_ref[1], axis=-1, keepdims=True)
    o = acc_ref[0] / l0 - lam * (acc_ref[1] / l1)
    o_ref[...] = _sub_norm(o, gsub_ref[...], lam_init).astype(o_ref.dtype)


def diff_attn_prompt(q, k, v, lam_rows, g_sub, lam_init, batch, seq, tq=512):
    M, D = q.shape
    H, DV = DIFF_HEADS, DIFF_DV
    tq = min(seq, tq)
    nq = seq // tq
    body = functools.partial(_diff_attn_body, tq=tq, lam_init=lam_init)
    return pl.pallas_call(
        body,
        grid=(batch, H, nq),
        in_specs=[pl.BlockSpec((tq, DV), lambda b, h, i: (b * nq + i, h)),
                  pl.BlockSpec((seq, DV), lambda b, h, i: (b, h)),
                  pl.BlockSpec((seq, DV), lambda b, h, i: (b, h)),
                  pl.BlockSpec((4, DIFF_DH), lambda b, h, i: (0, 0)),
                  pl.BlockSpec((1, DV), lambda b, h, i: (0, 0))],
        out_specs=pl.BlockSpec((tq, DV), lambda b, h, i: (b * nq + i, h)),
        out_shape=jax.ShapeDtypeStruct((M, D), BF16),
        scratch_shapes=[pltpu.VMEM((2, tq, LANES), F32),
                        pltpu.VMEM((2, tq, LANES), F32),
                        pltpu.VMEM((2, tq, DV), F32)],
        compiler_params=_params(3),
        name="diff_attn_prompt",
    )(q, k, v, lam_rows, g_sub.reshape(1, DV).astype(F32))


def _diff_decode_body(pt_ref, qt_ref, *refs, pages_per_step, lam_init):
    k_refs = refs[:pages_per_step]
    v_refs = refs[pages_per_step:2 * pages_per_step]
    knew_ref, vnew_ref, lam_ref, gsub_ref, o_ref, m_ref, l_ref, acc_ref = refs[2 * pages_per_step:]
    step = pl.program_id(1)
    n_steps = pl.num_programs(1)
    H = DIFF_HEADS
    NQ = 2 * H

    @pl.when(step == 0)
    def _():
        m_ref[...] = jnp.full(m_ref.shape, -jnp.inf, F32)
        l_ref[...] = jnp.zeros(l_ref.shape, F32)
        acc_ref[...] = jnp.zeros(acc_ref.shape, F32)

    qt = qt_ref[0]

    def update(kb, vb, n_real_rows):
        n_rows = kb.shape[0]
        st = lax.dot_general(qt, kb, (((1,), (1,)), ((), ())), preferred_element_type=F32)
        rr = lax.broadcasted_iota(jnp.int32, (NQ, n_rows), 0)
        cc = lax.broadcasted_iota(jnp.int32, (NQ, n_rows), 1)
        valid = (cc % H) == (rr % H)
        if n_real_rows < n_rows:
            valid = valid & (cc < n_real_rows)
        st = jnp.where(valid, st, -jnp.inf)
        m_prev = m_ref[...]
        m_new = jnp.maximum(m_prev, jnp.max(st, axis=-1, keepdims=True))
        corr = jnp.exp(m_prev - m_new)
        p = jnp.exp(st - m_new[:, :1])
        l_ref[...] = l_ref[...] * corr + jnp.sum(p, axis=-1, keepdims=True)
        acc_ref[...] = acc_ref[...] * corr[:, :1] + jnp.dot(p.astype(BF16), vb, preferred_element_type=F32)
        m_ref[...] = m_new

    for r in range(pages_per_step):
        update(k_refs[r][0].astype(BF16), v_refs[r][0].astype(BF16), PAGE_SIZE * H)

    @pl.when(step == n_steps - 1)
    def _():
        pad = jnp.zeros((LANES - H, knew_ref.shape[2]), F32)
        kn = jnp.concatenate([knew_ref[0], pad], axis=0).astype(BF16)
        vn = jnp.concatenate([vnew_ref[0], pad], axis=0).astype(BF16)
        update(kn, vn, H)
        lam = _lambda_value(lam_ref, lam_init)
        acc = acc_ref[...]
        l = l_ref[...]
        o = acc[0:H] / l[0:H, :1] - lam * (acc[H:NQ] / l[H:NQ, :1])
        o_ref[0] = _sub_norm(o, gsub_ref[...], lam_init)


def diff_attn_decode(q, k_new, v_new, cache_k, cache_v, page_table, lam_rows, g_sub, lam_init):
    B, D = q.shape
    H, DH, DV = DIFF_HEADS, DIFF_DH, DIFF_DV
    n_pool = cache_k.shape[0]
    n_pages = page_table.shape[1]
    pages_per_step = 2
    n_steps = n_pages // pages_per_step
    rows = PAGE_SIZE * H
    ck = cache_k.reshape(n_pool, rows, 2 * DH)
    cv = cache_v.reshape(n_pool, rows, DV)
    q4 = q.reshape(B, H, 2, DH)
    zero = jnp.zeros((B, H, DH), q.dtype)
    qt = jnp.concatenate([jnp.concatenate([q4[:, :, 0], zero], axis=-1),
                          jnp.concatenate([zero, q4[:, :, 1]], axis=-1)], axis=1).astype(BF16)

    def page_spec(r, width):
        return pl.BlockSpec((1, rows, width), lambda b, s, pt: (pt[b, s * pages_per_step + r], 0, 0))

    body = functools.partial(_diff_decode_body, pages_per_step=pages_per_step, lam_init=lam_init)
    grid_spec = pltpu.PrefetchScalarGridSpec(
        num_scalar_prefetch=1,
        grid=(B, n_steps),
        in_specs=[pl.BlockSpec((1, 2 * H, 2 * DH), lambda b, s, pt: (b, 0, 0))]
                 + [page_spec(r, 2 * DH) for r in range(pages_per_step)]
                 + [page_spec(r, DV) for r in range(pages_per_step)]
                 + [pl.BlockSpec((1, H, 2 * DH), lambda b, s, pt: (b, 0, 0)),
                    pl.BlockSpec((1, H, DV), lambda b, s, pt: (b, 0, 0)),
                    pl.BlockSpec((4, DH), lambda b, s, pt: (0, 0)),
                    pl.BlockSpec((1, DV), lambda b, s, pt: (0, 0))],
        out_specs=pl.BlockSpec((1, H, DV), lambda b, s, pt: (b, 0, 0)),
        scratch_shapes=[pltpu.VMEM((2 * H, LANES), F32),
                        pltpu.VMEM((2 * H, LANES), F32),
                        pltpu.VMEM((2 * H, DV), F32)],
    )
    out = pl.pallas_call(
        body,
        grid_spec=grid_spec,
        out_shape=jax.ShapeDtypeStruct((B, H, DV), F32),
        compiler_params=_params(2),
        name="diff_attn_decode",
    )(page_table, qt, *([ck] * pages_per_step), *([cv] * pages_per_step),
      k_new.reshape(B, H, 2 * DH), v_new.reshape(B, H, DV), lam_rows, g_sub.reshape(1, DV).astype(F32))
    return out.reshape(B, H * DV)


def kernel(x_prompt, x_sample, state_ret, cache_k, cache_v, page_table, g_ffn1, w_ffn1_gu, w_ffn1_down, g_mix, g_ffn2, w_ffn2_gu, w_ffn2_down, w_ret_qkvg, g_ret_gn, w_ret_o, g_kv, w_kv, g_k_norm, w_diff_q, g_q_norm, lambda_q1, lambda_k1, lambda_q2, lambda_k2, g_sub, w_diff_o):
    B, S, D = x_prompt.shape
    Bd, Sd, _ = x_sample.shape
    pos_p = jnp.arange(S)
    pos_s = jnp.broadcast_to(PAST_LEN + jnp.arange(Sd), (Bd * Sd,))
    xp = x_prompt.reshape(B * S, D)
    xs = x_sample.reshape(Bd * Sd, D)
    ret_p, ret_s = [], []
    kp32 = vp32 = ks32 = vs32 = kp = vp = None
    for i in range(DEPTH):
        if i == N_A_LAYERS:
            (kp32, kp, vp32, vp), (ks32, vs32) = shared_kv(xp, xs, g_kv, (w_kv[None], 0), g_k_norm, pos_p, pos_s)
        xp, xs = swiglu_half_step(xp, xs, g_ffn1[i], (w_ffn1_gu, i), (w_ffn1_down, i))
        if i < N_A_LAYERS:
            a = i
            (qk, v, g), (qk_s, v_s, g_s) = ret_project(
                rmsnorm_cast(xp, g_mix[i], BF16), rmsnorm_cast(xs, g_mix[i], F32), (w_ret_qkvg, a), pos_p, pos_s)
            y, s_fin = retention_prompt(qk, v, g, g_ret_gn[a], B, S)
            y_s, s_new = retention_step(qk_s, v_s, g_s, g_ret_gn[a], state_ret[a])
            xp, xs = matmul_residual(y, y_s, (w_ret_o, a), xp, xs, 1.0, name="ret_out_proj")
            ret_p.append(s_fin)
            ret_s.append(s_new)
        else:
            bidx = i - N_A_LAYERS
            lam_init = 0.8 - 0.6 * math.exp(-0.3 * i)
            lam_rows = jnp.stack([lambda_q1[bidx], lambda_k1[bidx],
                                  lambda_q2[bidx], lambda_k2[bidx]]).astype(F32)
            q, q_s = diff_query(rmsnorm_cast(xp, g_mix[i], BF16), rmsnorm_cast(xs, g_mix[i], F32),
                                (w_diff_q, bidx), g_q_norm[bidx], pos_p, pos_s)
            o = diff_attn_prompt(q, kp, vp, lam_rows, g_sub[bidx], lam_init, B, S)
            o_s = diff_attn_decode(q_s, ks32, vs32, cache_k, cache_v, page_table, lam_rows, g_sub[bidx], lam_init)
            xp, xs = matmul_residual(o, o_s, (w_diff_o, bidx), xp, xs, 1.0, name="diff_out_proj")
        xp, xs = swiglu_half_step(xp, xs, g_ffn2[i], (w_ffn2_gu, i), (w_ffn2_down, i))
    return (xp.reshape(B, S, D), xs.reshape(Bd, Sd, D),
            jnp.stack(ret_p), jnp.stack(ret_s),
            kp32.reshape(B, S, DIFF_HEADS, 2 * DIFF_DH), vp32.reshape(B, S, DIFF_HEADS, DIFF_DV),
            ks32.reshape(Bd, Sd, DIFF_HEADS, 2 * DIFF_DH), vs32.reshape(Bd, Sd, DIFF_HEADS, DIFF_DV))
```
